```python
import math
import jax, jax.numpy as jnp
from jax import lax
import numpy as np

D_MODEL = 1024
BATCH = 8
SEQ = 4096
DEPTH = 2

N_A_LAYERS = DEPTH // 2
N_B_LAYERS = DEPTH - N_A_LAYERS

GDN_HEADS = 8
GDN_DK = 128
GDN_DV = 128
GDN_QK_WIDTH = GDN_HEADS * GDN_DK
GDN_V_WIDTH = GDN_HEADS * GDN_DV
GDN_CONV = 4
GDN_CHUNK = 64
GDN_IN_WIDTH = 2 * GDN_QK_WIDTH + 2 * GDN_V_WIDTH + 2 * GDN_HEADS

FOX_HEADS = 16
FOX_HD = 64
FOX_WIDTH = FOX_HEADS * FOX_HD
FOX_QBLOCK = 128
KV_WIDTH = 2 * FOX_WIDTH + FOX_HEADS

D_FF = int(math.ceil(8 * D_MODEL / 3 / 256) * 256)

RMS_EPS = 1e-6

kernel_name = "yoco_gdn_fox_hybrid"


def rms_norm(x, w):
    xf = x.astype(jnp.float32)
    y = xf * lax.rsqrt(jnp.mean(xf * xf, axis=-1, keepdims=True) + RMS_EPS)
    return (y * w.astype(jnp.float32)).astype(x.dtype)


def l2_normalize(x):
    xf = x.astype(jnp.float32)
    return xf * lax.rsqrt(jnp.sum(xf * xf, axis=-1, keepdims=True) + RMS_EPS)


def causal_depthwise_conv_silu(x, w):
    k_width = w.shape[0]
    y = lax.conv_general_dilated(
        x, w[:, None, :].astype(x.dtype), window_strides=(1,), padding=[(k_width - 1, 0)],
        dimension_numbers=("NWC", "WIO", "NWC"), feature_group_count=x.shape[-1])
    return jax.nn.silu(y)


def gated_delta_rule_chunked(q, k, v, g, beta):
    b_sz, t_len, n_h, d_k = q.shape
    d_v = v.shape[-1]
    c = GDN_CHUNK
    n_c = t_len // c

    def to_chunks(a):
        return a.reshape(b_sz, n_c, c, n_h, *a.shape[3:]).swapaxes(2, 3)

    qc, kc, vc, gc, bc = map(to_chunks, (q, k, v, g, beta))
    G = jnp.cumsum(gc, axis=-1)
    causal = jnp.tril(jnp.ones((c, c), dtype=bool))
    strict = jnp.tril(jnp.ones((c, c), dtype=bool), -1)
    diff = G[..., :, None] - G[..., None, :]
    decay_mat = jnp.where(causal, jnp.exp(jnp.where(causal, diff, 0.0)), 0.0)

    kk = jnp.einsum("bnhcd,bnhsd->bnhcs", kc, kc)
    a_strict = jnp.where(strict, bc[..., :, None] * kk * decay_mat, 0.0)
    rhs = jnp.concatenate([bc[..., None] * kc * jnp.exp(G)[..., None], bc[..., None] * vc], axis=-1)
    sol = lax.linalg.triangular_solve(a_strict, rhs, left_side=True, lower=True, unit_diagonal=True)
    w_c, u_c = sol[..., :d_k], sol[..., d_k:]

    a_qk = jnp.einsum("bnhcd,bnhsd->bnhcs", qc, kc) * decay_mat
    g_last = G[..., -1]
    q_dec = qc * jnp.exp(G)[..., None]
    k_dec = kc * jnp.exp(g_last[..., None] - G)[..., None]

    xs = tuple(jnp.moveaxis(a, 1, 0) for a in (q_dec, k_dec, u_c, w_c, a_qk, jnp.exp(g_last)))

    def step(state, inp):
        qg, kd, u_i, w_i, a_i, gl = inp
        v_new = u_i - jnp.einsum("bhcd,bhde->bhce", w_i, state)
        o = jnp.einsum("bhcd,bhde->bhce", qg, state) + jnp.einsum("bhcs,bhse->bhce", a_i, v_new)
        state = state * gl[..., None, None] + jnp.einsum("bhcd,bhce->bhde", kd, v_new)
        return state, o

    s0 = jnp.zeros((b_sz, n_h, d_k, d_v), jnp.float32)
    _, o = lax.scan(step, s0, xs)
    return o.transpose(1, 0, 3, 2, 4).reshape(b_sz, t_len, n_h, d_v)


def gated_deltanet_mixer(h, w_in, conv_w, a_log, dt_bias, out_norm, w_out):
    b_sz, t_len, _ = h.shape
    proj = h @ w_in
    o1 = 2 * GDN_QK_WIDTH + GDN_V_WIDTH
    o2 = o1 + GDN_V_WIDTH
    qkv = causal_depthwise_conv_silu(proj[..., :o1], conv_w)
    z = proj[..., o1:o2].reshape(b_sz, t_len, GDN_HEADS, GDN_DV)
    b_logit = proj[..., o2:o2 + GDN_HEADS]
    a_in = proj[..., o2 + GDN_HEADS:]
    q = qkv[..., :GDN_QK_WIDTH].reshape(b_sz, t_len, GDN_HEADS, GDN_DK)
    k = qkv[..., GDN_QK_WIDTH:2 * GDN_QK_WIDTH].reshape(b_sz, t_len, GDN_HEADS, GDN_DK)
    v = qkv[..., 2 * GDN_QK_WIDTH:].reshape(b_sz, t_len, GDN_HEADS, GDN_DV).astype(jnp.float32)
    q = l2_normalize(q) * (GDN_DK ** -0.5)
    k = l2_normalize(k)
    beta = jax.nn.sigmoid(b_logit.astype(jnp.float32))
    g = -jnp.exp(a_log.astype(jnp.float32)) * jax.nn.softplus(
        a_in.astype(jnp.float32) + dt_bias.astype(jnp.float32))
    o = gated_delta_rule_chunked(q, k, v, g, beta).astype(h.dtype)
    o = rms_norm(o, out_norm) * jax.nn.silu(z)
    return o.reshape(b_sz, t_len, GDN_V_WIDTH) @ w_out


def shared_kv(x, kv_norm, w_kv, b_forget):
    b_sz, t_len, _ = x.shape
    kvf = rms_norm(x, kv_norm) @ w_kv
    k = kvf[..., :FOX_WIDTH].reshape(b_sz, t_len, FOX_HEADS, FOX_HD).transpose(0, 2, 1, 3)
    v = kvf[..., FOX_WIDTH:2 * FOX_WIDTH].reshape(b_sz, t_len, FOX_HEADS, FOX_HD).transpose(0, 2, 1, 3)
    f_logit = (kvf[..., 2 * FOX_WIDTH:] + b_forget).astype(jnp.float32)
    log_f_cum = jnp.cumsum(jax.nn.log_sigmoid(f_logit), axis=1).transpose(0, 2, 1)
    return k, v, log_f_cum


def forgetting_attention_mixer(h, w_q, w_o, k, v, log_f_cum):
    b_sz, t_len, _ = h.shape
    q = (h @ w_q).reshape(b_sz, t_len, FOX_HEADS, FOX_HD).transpose(0, 2, 1, 3)
    scale = FOX_HD ** -0.5
    outs = []
    for blk in range(t_len // FOX_QBLOCK):
        s0 = blk * FOX_QBLOCK
        e = s0 + FOX_QBLOCK
        s = jnp.einsum("bhqd,bhkd->bhqk", q[:, :, s0:e], k[:, :, :e]).astype(jnp.float32) * scale
        s = s + log_f_cum[:, :, s0:e, None] - log_f_cum[:, :, None, :e]
        mask = (s0 + jnp.arange(FOX_QBLOCK))[:, None] >= jnp.arange(e)[None, :]
        p = jax.nn.softmax(jnp.where(mask, s, -jnp.inf), axis=-1)
        outs.append(jnp.einsum("bhqk,bhkd->bhqd", p.astype(v.dtype), v[:, :, :e]))
    o = jnp.concatenate(outs, axis=2)
    return o.transpose(0, 2, 1, 3).reshape(b_sz, t_len, FOX_WIDTH) @ w_o


def swiglu(h, w_in, w_out):
    gu = h @ w_in
    return (jax.nn.silu(gu[..., :D_FF]) * gu[..., D_FF:]) @ w_out


def setup_inputs(seed: int = 0) -> dict:
    key = jax.random.key(seed)
    ks = jax.random.split(key, 20)
    f32 = jnp.float32

    def w(k, shape, fan_in):
        return jax.random.normal(k, shape, f32) * (fan_in ** -0.5)

    def gain(k, shape):
        return 1.0 + 0.1 * jax.random.normal(k, shape, f32)

    x = jax.random.normal(ks[0], (BATCH, SEQ, D_MODEL), f32)
    a_init = jax.random.uniform(ks[9], (N_A_LAYERS, GDN_HEADS), f32, 1.0, 16.0)
    dt = jnp.exp(jax.random.uniform(ks[10], (N_A_LAYERS, GDN_HEADS), f32, math.log(1e-3), math.log(0.1)))
    return {
        "x": x,
        "pre_mix_norm": gain(ks[1], (DEPTH, D_MODEL)),
        "post_mix_norm": gain(ks[2], (DEPTH, D_MODEL)),
        "pre_ffn_norm": gain(ks[3], (DEPTH, D_MODEL)),
        "post_ffn_norm": gain(ks[4], (DEPTH, D_MODEL)),
        "w_ffn_in": w(ks[5], (DEPTH, D_MODEL, 2 * D_FF), D_MODEL),
        "w_ffn_out": w(ks[6], (DEPTH, D_FF, D_MODEL), D_FF),
        "gdn_w_in": w(ks[7], (N_A_LAYERS, D_MODEL, GDN_IN_WIDTH), D_MODEL),
        "gdn_conv": w(ks[8], (N_A_LAYERS, GDN_CONV, 2 * GDN_QK_WIDTH + GDN_V_WIDTH), GDN_CONV),
        "gdn_a_log": jnp.log(a_init),
        "gdn_dt_bias": dt + jnp.log(-jnp.expm1(-dt)),
        "gdn_out_norm": gain(ks[11], (N_A_LAYERS, GDN_DV)),
        "gdn_w_out": w(ks[12], (N_A_LAYERS, GDN_V_WIDTH, D_MODEL), GDN_V_WIDTH),
        "kv_norm": gain(ks[13], (D_MODEL,)),
        "w_kv": w(ks[14], (D_MODEL, KV_WIDTH), D_MODEL),
        "b_forget": 2.0 + 0.5 * jax.random.normal(ks[15], (FOX_HEADS,), f32),
        "fox_w_q": w(ks[16], (N_B_LAYERS, D_MODEL, FOX_WIDTH), D_MODEL),
        "fox_w_o": w(ks[17], (N_B_LAYERS, FOX_WIDTH, D_MODEL), FOX_WIDTH),
    }


def reference(x, pre_mix_norm, post_mix_norm, pre_ffn_norm, post_ffn_norm, w_ffn_in, w_ffn_out,
              gdn_w_in, gdn_conv, gdn_a_log, gdn_dt_bias, gdn_out_norm, gdn_w_out,
              kv_norm, w_kv, b_forget, fox_w_q, fox_w_o):
    k_sh = v_sh = c_sh = None
    for layer in range(DEPTH):
        h = rms_norm(x, pre_mix_norm[layer])
        if layer < N_A_LAYERS:
            i = layer
            mix = gated_deltanet_mixer(h, gdn_w_in[i], gdn_conv[i], gdn_a_log[i], gdn_dt_bias[i],
                                       gdn_out_norm[i], gdn_w_out[i])
        else:
            i = layer - N_A_LAYERS
            if i == 0:
                k_sh, v_sh, c_sh = shared_kv(x, kv_norm, w_kv, b_forget)
            mix = forgetting_attention_mixer(h, fox_w_q[i], fox_w_o[i], k_sh, v_sh, c_sh)
        x = x + rms_norm(mix, post_mix_norm[layer])
        f = swiglu(rms_norm(x, pre_ffn_norm[layer]), w_ffn_in[layer], w_ffn_out[layer])
        x = x + rms_norm(f, post_ffn_norm[layer])
    return x
```

```python
import functools
import math

import jax
import jax.numpy as jnp
from jax import lax
from jax.experimental import pallas as pl
from jax.experimental.pallas import tpu as pltpu

F32 = jnp.float32
BF16 = jnp.bfloat16

RMS_EPS = 1e-6
LANES = 128
GDN_HEADS = 8
GDN_DK = 128
GDN_DV = 128
GDN_CONV = 4
GDN_CHUNK = 64
FOX_HEADS = 16
FOX_HD = 64
NEG_BIG = -1e30

VMEM_LIMIT = 56 * 1024 * 1024


def _cparams(*sem):
    return pltpu.CompilerParams(dimension_semantics=sem, vmem_limit_bytes=VMEM_LIMIT)


def _mm(a, b):
    return jnp.dot(a.astype(BF16), b.astype(BF16), preferred_element_type=F32)


def _mm_nt(a, b):
    return lax.dot_general(a.astype(BF16), b.astype(BF16), (((1,), (1,)), ((), ())),
                           preferred_element_type=F32)


def _mm_tn(a, b):
    return lax.dot_general(a.astype(BF16), b.astype(BF16), (((0,), (0,)), ((), ())),
                           preferred_element_type=F32)


def _rms_scale(x):
    return lax.rsqrt(jnp.mean(x * x, axis=-1, keepdims=True) + RMS_EPS)


def _sigmoid(x):
    return 1.0 / (1.0 + jnp.exp(-x))


def _softplus(x):
    return jnp.maximum(x, 0.0) + jnp.log(1.0 + jnp.exp(-jnp.abs(x)))


def _resident(shape):
    nd = len(shape)
    return pl.BlockSpec(shape, lambda *_: (0,) * nd, pipeline_mode=pl.Buffered(1))


def _gdn_in_proj_kernel(x_ref, g_ref, wm_ref, ws_ref, om_ref, os_ref):
    x = x_ref[...]
    h = (x * _rms_scale(x) * g_ref[...]).astype(BF16)
    om_ref[...] = jnp.dot(h, wm_ref[...], preferred_element_type=F32)
    os_ref[...] = jnp.dot(h, ws_ref[...], preferred_element_type=F32)


def _gdn_in_proj(x2, gain, w_main, w_small, tm):
    n, d = x2.shape
    nm, ns = w_main.shape[1], w_small.shape[1]
    return pl.pallas_call(
        _gdn_in_proj_kernel,
        grid=(n // tm,),
        in_specs=[pl.BlockSpec((tm, d), lambda i: (i, 0)),
                  _resident((1, d)), _resident((d, nm)), _resident((d, ns))],
        out_specs=[pl.BlockSpec((tm, nm), lambda i: (i, 0)),
                   pl.BlockSpec((tm, ns), lambda i: (i, 0))],
        out_shape=[jax.ShapeDtypeStruct((n, nm), F32), jax.ShapeDtypeStruct((n, ns), F32)],
        compiler_params=_cparams("parallel"),
        name="gdn_in_proj",
    )(x2, gain, w_main, w_small)


def _fox_proj_kernel(x_ref, gq_ref, gkv_ref, wq_ref, wk_ref, wv_ref, wf_ref, bf_ref,
                     q_ref, k_ref, v_ref, lf_ref):
    x = x_ref[...]
    xn = x * _rms_scale(x)
    hq = (xn * gq_ref[...]).astype(BF16)
    hkv = (xn * gkv_ref[...]).astype(BF16)
    scale = FOX_HD ** -0.5
    q_ref[...] = (jnp.dot(hq, wq_ref[...], preferred_element_type=F32) * scale).astype(BF16)
    k_ref[...] = jnp.dot(hkv, wk_ref[...], preferred_element_type=F32).astype(BF16)
    v_ref[...] = jnp.dot(hkv, wv_ref[...], preferred_element_type=F32).astype(BF16)
    f = jnp.dot(hkv, wf_ref[...], preferred_element_type=F32) + bf_ref[...]
    lf_ref[...] = -_softplus(-f)


def _fox_proj(x2, gq, gkv, wq, wk, wv, wf, bf, tm):
    n, d = x2.shape
    w = wq.shape[1]
    row = lambda i: (i, 0)
    return pl.pallas_call(
        _fox_proj_kernel,
        grid=(n // tm,),
        in_specs=[pl.BlockSpec((tm, d), row), _resident((1, d)), _resident((1, d)),
                  _resident((d, w)), _resident((d, w)), _resident((d, w)),
                  _resident((d, LANES)), _resident((1, LANES))],
        out_specs=[pl.BlockSpec((tm, w), row), pl.BlockSpec((tm, w), row),
                   pl.BlockSpec((tm, w), row), pl.BlockSpec((tm, LANES), row)],
        out_shape=[jax.ShapeDtypeStruct((n, w), BF16), jax.ShapeDtypeStruct((n, w), BF16),
                   jax.ShapeDtypeStruct((n, w), BF16), jax.ShapeDtypeStruct((n, LANES), F32)],
        compiler_params=_cparams("parallel"),
        name="fox_proj",
    )(x2, gq, gkv, wq, wk, wv, wf, bf)


def _out_proj_kernel(a_ref, w_ref, x_ref, g_ref, o_ref):
    y = jnp.dot(a_ref[...], w_ref[...], preferred_element_type=F32)
    o_ref[...] = x_ref[...] + y * _rms_scale(y) * g_ref[...]


def _out_proj(a, w, x2, gain, tm):
    n, d = x2.shape
    k = a.shape[1]
    row = lambda i: (i, 0)
    return pl.pallas_call(
        _out_proj_kernel,
        grid=(n // tm,),
        in_specs=[pl.BlockSpec((tm, k), row), _resident((k, d)),
                  pl.BlockSpec((tm, d), row), _resident((1, d))],
        out_specs=pl.BlockSpec((tm, d), row),
        out_shape=jax.ShapeDtypeStruct((n, d), F32),
        compiler_params=_cparams("parallel"),
        name="out_proj",
    )(a, w, x2, gain)


def _ffn_kernel(x_ref, g1_ref, wg_ref, wu_ref, wo_ref, g2_ref, o_ref):
    x = x_ref[...]
    h = (x * _rms_scale(x) * g1_ref[...]).astype(BF16)
    g = jnp.dot(h, wg_ref[...], preferred_element_type=F32)
    u = jnp.dot(h, wu_ref[...], preferred_element_type=F32)
    a = (g * _sigmoid(g) * u).astype(BF16)
    f = jnp.dot(a, wo_ref[...], preferred_element_type=F32)
    o_ref[...] = x + f * _rms_scale(f) * g2_ref[...]


def _ffn(x2, g1, wg, wu, wo, g2, tm):
    n, d = x2.shape
    dff = wg.shape[1]
    row = lambda i: (i, 0)
    return pl.pallas_call(
        _ffn_kernel,
        grid=(n // tm,),
        in_specs=[pl.BlockSpec((tm, d), row), _resident((1, d)), _resident((d, dff)),
                  _resident((d, dff)), _resident((dff, d)), _resident((1, d))],
        out_specs=pl.BlockSpec((tm, d), row),
        out_shape=jax.ShapeDtypeStruct((n, d), F32),
        compiler_params=_cparams("parallel"),
        name="ffn",
    )(x2, g1, wg, wu, wo, g2)


def _tri_inverse(a, eye, level_masks):
    d = eye - a * level_masks[0]
    for m in level_masks[1:]:
        x = _mm(a * m, d)
        d = d - _mm(d, x)
    return d


def _gdn_kernel(qkv_ref, z_ref, gate_ref, cw_ref, alog_ref, dtb_ref, onorm_ref, o_ref,
                s_ref, tail_ref, q_s, k_s, v_s, beta_s, gcum_s, *, tb):
    c = GDN_CHUNK
    n_chunks = tb // c
    qk_w = GDN_HEADS * GDN_DK

    @pl.when(pl.program_id(1) == 0)
    def _():
        s_ref[...] = jnp.zeros_like(s_ref)
        tail_ref[...] = jnp.zeros_like(tail_ref)

    for j in range(3 * GDN_HEADS):
        cols = slice(j * LANES, (j + 1) * LANES)
        xx = jnp.concatenate([tail_ref[:, cols], qkv_ref[:, cols]], axis=0)
        y = None
        for kk in range(GDN_CONV):
            sh = GDN_CONV - 1 - kk
            xs = xx if sh == 0 else pltpu.roll(xx, sh, axis=0)
            term = xs[8:8 + tb] * cw_ref[kk:kk + 1, cols]
            y = term if y is None else y + term
        y = y * _sigmoid(y)
        if j < GDN_HEADS:
            y = y * lax.rsqrt(jnp.sum(y * y, axis=-1, keepdims=True) + RMS_EPS) * (GDN_DK ** -0.5)
            q_s[:, cols] = y
        elif j < 2 * GDN_HEADS:
            y = y * lax.rsqrt(jnp.sum(y * y, axis=-1, keepdims=True) + RMS_EPS)
            k_s[:, slice((j - GDN_HEADS) * LANES, (j - GDN_HEADS + 1) * LANES)] = y
        else:
            v_s[:, slice((j - 2 * GDN_HEADS) * LANES, (j - 2 * GDN_HEADS + 1) * LANES)] = y
    tail_ref[...] = qkv_ref[tb - 8:tb, :]

    gates = gate_ref[...]
    beta_s[...] = _sigmoid(gates)
    g = -jnp.exp(alog_ref[...]) * _softplus(gates + dtb_ref[...])
    row = lax.broadcasted_iota(jnp.int32, (tb, LANES), 0) & (c - 1)
    sh = 1
    while sh < c:
        g = g + jnp.where(row >= sh, pltpu.roll(g, sh, axis=0), 0.0)
        sh *= 2
    gcum_s[...] = g

    ri = lax.broadcasted_iota(jnp.int32, (c, c), 0)
    ci = lax.broadcasted_iota(jnp.int32, (c, c), 1)
    causal = ri >= ci
    strict = ri > ci
    eye = (ri == ci).astype(F32)
    level_masks = []
    s = 1
    while s < c:
        blk = ~(2 * s - 1)
        m = ((ri & blk) == (ci & blk)) & ((ri & (2 * s - 1)) >= s) & ((ci & (2 * s - 1)) < s)
        level_masks.append(m.astype(F32))
        s *= 2

    def chunk_body(ic, carry):
        r0 = pl.multiple_of(ic * c, c)
        rows = pl.ds(r0, c)
        bt = beta_s[rows, :]
        gc = gcum_s[rows, :]
        g_last = gc[c - 1:c, :]
        e_all = jnp.exp(gc)
        dk_all = jnp.exp(g_last - gc)
        gl_all = jnp.exp(g_last)
        gc_t = jnp.transpose(jnp.concatenate([gc, jnp.zeros_like(gc)], axis=0))
        for h in range(GDN_HEADS):
            cols = slice(h * LANES, (h + 1) * LANES)
            qh = q_s[rows, cols]
            kh = k_s[rows, cols]
            vh = v_s[rows, cols]
            b_col = bt[:, h:h + 1]
            g_col = gc[:, GDN_HEADS + h:GDN_HEADS + h + 1]
            e_col = e_all[:, GDN_HEADS + h:GDN_HEADS + h + 1]
            dk_col = dk_all[:, GDN_HEADS + h:GDN_HEADS + h + 1]
            g_row = gc_t[GDN_HEADS + h:GDN_HEADS + h + 1, :c]
            diff = g_col - g_row
            decay = jnp.where(causal, jnp.exp(jnp.where(causal, diff, 0.0)), 0.0)
            kb = kh.astype(BF16)
            kk_m = _mm_nt(kb, kb)
            a_strict = jnp.where(strict, b_col * kk_m * decay, 0.0)
            t_inv = _tri_inverse(a_strict, eye, level_masks)
            rhs = jnp.concatenate([(b_col * e_col) * kh, b_col * vh], axis=1)
            sol = _mm(t_inv, rhs)
            w_c = sol[:, :GDN_DK]
            u_c = sol[:, GDN_DK:]
            a_qk = _mm_nt(qh, kb) * decay
            q_dec = qh * e_col
            k_dec = kh * dk_col
            st = s_ref[h]
            wq = _mm(jnp.concatenate([w_c, q_dec], axis=0), st)
            v_new = u_c - wq[:c]
            o = wq[c:] + _mm(a_qk, v_new)
            gl = gl_all[0:1, GDN_HEADS + h:GDN_HEADS + h + 1]
            s_ref[h] = st * gl + _mm_tn(k_dec, v_new)
            zh = z_ref[rows, cols]
            on = o * _rms_scale(o) * onorm_ref[...]
            o_ref[rows, cols] = (on * (zh * _sigmoid(zh))).astype(o_ref.dtype)
        return carry

    lax.fori_loop(0, n_chunks, chunk_body, 0)


def _gdn_core(proj_main, proj_small, conv_w, alog_pad, dtb_pad, out_norm, tb):
    b, t, _ = proj_main.shape
    qkv_w = 3 * GDN_HEADS * GDN_DK
    v_w = GDN_HEADS * GDN_DV
    return pl.pallas_call(
        functools.partial(_gdn_kernel, tb=tb),
        grid=(b, t // tb),
        in_specs=[pl.BlockSpec((None, tb, qkv_w), lambda i, j: (i, j, 0)),
                  pl.BlockSpec((None, tb, v_w), lambda i, j: (i, j, qkv_w // v_w)),
                  pl.BlockSpec((None, tb, LANES), lambda i, j: (i, j, 0)),
                  _resident((GDN_CONV, qkv_w)), _resident((1, LANES)), _resident((1, LANES)),
                  _resident((1, GDN_DV))],
        out_specs=pl.BlockSpec((None, tb, v_w), lambda i, j: (i, j, 0)),
        out_shape=jax.ShapeDtypeStruct((b, t, v_w), BF16),
        scratch_shapes=[pltpu.VMEM((GDN_HEADS, GDN_DK, GDN_DV), F32),
                        pltpu.VMEM((8, qkv_w), F32),
                        pltpu.VMEM((tb, GDN_HEADS * GDN_DK), F32),
                        pltpu.VMEM((tb, GDN_HEADS * GDN_DK), F32),
                        pltpu.VMEM((tb, v_w), F32),
                        pltpu.VMEM((tb, LANES), F32),
                        pltpu.VMEM((tb, LANES), F32)],
        compiler_params=_cparams("parallel", "arbitrary"),
        name="gdn_core",
    )(proj_main, proj_main, proj_small, conv_w, alog_pad, dtb_pad, out_norm)


def _cumsum_kernel(x_ref, o_ref):
    x = x_ref[...]
    t = x.shape[0]
    row = lax.broadcasted_iota(jnp.int32, x.shape, 0)
    sh = 1
    while sh < t:
        x = x + jnp.where(row >= sh, pltpu.roll(x, sh, axis=0), 0.0)
        sh *= 2
    o_ref[...] = x


def _time_cumsum(lf):
    b, t, w = lf.shape
    return pl.pallas_call(
        _cumsum_kernel,
        grid=(b,),
        in_specs=[pl.BlockSpec((None, t, w), lambda i: (i, 0, 0))],
        out_specs=pl.BlockSpec((None, t, w), lambda i: (i, 0, 0)),
        out_shape=jax.ShapeDtypeStruct((b, t, w), F32),
        compiler_params=_cparams("parallel"),
        name="forget_cumsum",
    )(lf)


def _attn_kernel(q_ref, k_ref, v_ref, cq_ref, ck_ref, o_ref, *, tq):
    pair = pl.program_id(1)
    qi = pl.program_id(2)
    q2 = q_ref[...]
    lane = lax.broadcasted_iota(jnp.int32, (1, LANES), 1)
    cq_all = cq_ref[...]
    ri = lax.broadcasted_iota(jnp.int32, (tq, tq), 0)
    ci = lax.broadcasted_iota(jnp.int32, (tq, tq), 1)
    causal = ri >= ci
    outs = []
    for p in range(2):
        head_lanes = (lane >> 6) == p
        qp = jnp.where(head_lanes, q2, jnp.zeros_like(q2))
        cq = jnp.sum(jnp.where(lane == 2 * pair + p, cq_all, 0.0), axis=-1, keepdims=True)

        def step(j, carry, masked, p=p, qp=qp, cq=cq):
            m, l, acc = carry
            k0 = pl.multiple_of(j * tq, tq)
            kb = k_ref[pl.ds(k0, tq), :]
            vb = v_ref[pl.ds(k0, tq), :]
            ck = ck_ref[p:p + 1, pl.ds(k0, tq)]
            s = lax.dot_general(qp, kb, (((1,), (1,)), ((), ())), preferred_element_type=F32)
            s = s + (cq - ck)
            if masked:
                s = jnp.where(causal, s, NEG_BIG)
            m_new = jnp.maximum(m, jnp.max(s, axis=-1, keepdims=True))
            alpha = jnp.exp(m - m_new)
            pm = jnp.exp(s - m_new)
            l = alpha * l + jnp.sum(pm, axis=-1, keepdims=True)
            acc = alpha * acc + jnp.dot(pm.astype(BF16), vb, preferred_element_type=F32)
            return m_new, l, acc

        init = (jnp.full((tq, 1), NEG_BIG, F32), jnp.zeros((tq, 1), F32),
                jnp.zeros((tq, LANES), F32))
        carry = lax.fori_loop(0, qi, functools.partial(step, masked=False), init)
        m, l, acc = step(qi, carry, True)
        outs.append(acc / l)
    o_ref[...] = jnp.where((lane >> 6) == 0, outs[0], outs[1]).astype(o_ref.dtype)


def _fox_attention(q, k, v, c_col, c_row, tq):
    b, t, w = q.shape
    n_pairs = w // LANES
    return pl.pallas_call(
        functools.partial(_attn_kernel, tq=tq),
        grid=(b, n_pairs, t // tq),
        in_specs=[pl.BlockSpec((None, tq, LANES), lambda i, p, j: (i, j, p)),
                  pl.BlockSpec((None, t, LANES), lambda i, p, j: (i, 0, p)),
                  pl.BlockSpec((None, t, LANES), lambda i, p, j: (i, 0, p)),
                  pl.BlockSpec((None, tq, LANES), lambda i, p, j: (i, j, 0)),
                  pl.BlockSpec((None, None, 2, t), lambda i, p, j: (i, p, 0, 0))],
        out_specs=pl.BlockSpec((None, tq, LANES), lambda i, p, j: (i, j, p)),
        out_shape=jax.ShapeDtypeStruct((b, t, w), BF16),
        compiler_params=_cparams("parallel", "parallel", "arbitrary"),
        name="fox_attention",
    )(q, k, v, c_col, c_row)


def _pad_lanes(a):
    return jnp.pad(a, [(0, 0)] * (a.ndim - 1) + [(0, LANES - a.shape[-1])])


def kernel(x, pre_mix_norm, post_mix_norm, pre_ffn_norm, post_ffn_norm, w_ffn_in, w_ffn_out,
           gdn_w_in, gdn_conv, gdn_a_log, gdn_dt_bias, gdn_out_norm, gdn_w_out,
           kv_norm, w_kv, b_forget, fox_w_q, fox_w_o):
    b, t, d = x.shape
    n = b * t
    d_ff = w_ffn_out.shape[1]
    tm = 512
    x2 = x.reshape(n, d)
    row = lambda a: a.reshape(1, -1).astype(F32)

    def ffn(x2, layer):
        w_in = w_ffn_in[layer].astype(BF16)
        return _ffn(x2, row(pre_ffn_norm[layer]), w_in[:, :d_ff], w_in[:, d_ff:],
                    w_ffn_out[layer].astype(BF16), row(post_ffn_norm[layer]), tm)

    qkvz_w = 3 * GDN_HEADS * GDN_DK + GDN_HEADS * GDN_DV
    w_in = gdn_w_in[0]
    w_main = w_in[:, :qkvz_w].astype(BF16)
    w_small = _pad_lanes(w_in[:, qkvz_w:]).astype(BF16)
    proj_main, proj_small = _gdn_in_proj(x2, row(pre_mix_norm[0]), w_main, w_small, tm)
    zeros_h = jnp.zeros((GDN_HEADS,), F32)
    alog_pad = _pad_lanes(jnp.concatenate([zeros_h, gdn_a_log[0].astype(F32)])[None, :])
    dtb_pad = _pad_lanes(jnp.concatenate([zeros_h, gdn_dt_bias[0].astype(F32)])[None, :])
    og = _gdn_core(proj_main.reshape(b, t, qkvz_w), proj_small.reshape(b, t, LANES),
                   gdn_conv[0].astype(F32), alog_pad, dtb_pad, row(gdn_out_norm[0]), 256)
    x2 = _out_proj(og.reshape(n, -1), gdn_w_out[0].astype(BF16), x2, row(post_mix_norm[0]), tm)
    x2 = ffn(x2, 0)

    fw = FOX_HEADS * FOX_HD
    q, k, v, lf = _fox_proj(x2, row(pre_mix_norm[1]), row(kv_norm), fox_w_q[0].astype(BF16),
                            w_kv[:, :fw].astype(BF16), w_kv[:, fw:2 * fw].astype(BF16),
                            _pad_lanes(w_kv[:, 2 * fw:]).astype(BF16),
                            _pad_lanes(b_forget.astype(F32)[None, :]), tm)
    c_col = _time_cumsum(lf.reshape(b, t, LANES))
    c_row = jnp.swapaxes(c_col[:, :, :FOX_HEADS], 1, 2).reshape(b, FOX_HEADS // 2, 2, t)
    oa = _fox_attention(q.reshape(b, t, fw), k.reshape(b, t, fw), v.reshape(b, t, fw),
                        c_col, c_row, 256)
    x2 = _out_proj(oa.reshape(n, fw), fox_w_o[0].astype(BF16), x2, row(post_mix_norm[1]), tm)
    x2 = ffn(x2, 1)
    return x2.reshape(b, t, d)
```

```python
import functools
import math

import jax
import jax.numpy as jnp
import numpy as np
from jax import lax
from jax.experimental import pallas as pl
from jax.experimental.pallas import tpu as pltpu

F32 = jnp.float32
BF16 = jnp.bfloat16

RMS_EPS = 1e-6
LANES = 128
GDN_HEADS = 8
GDN_DK = 128
GDN_DV = 128
GDN_CONV = 4
GDN_CHUNK = 64
FOX_HEADS = 16
FOX_HD = 64
NEG_BIG = -1e30
LOG2_E = math.log2(math.e)
BIAS_TERMS = 6

ROW_TILE = 512
GDN_TIME_BLOCK = 256
GDN_CHUNKS_PER_PREP = 2
ATTN_TILE = 512
ATTN_ROW_CHUNK = 1024

VMEM_LIMIT = 56 * 1024 * 1024


def _cparams(*sem):
    return pltpu.CompilerParams(dimension_semantics=sem, vmem_limit_bytes=VMEM_LIMIT)


def _mm(a, b):
    return jnp.dot(a.astype(BF16), b.astype(BF16), preferred_element_type=F32)


def _mm_nt(a, b):
    return lax.dot_general(a.astype(BF16), b.astype(BF16), (((1,), (1,)), ((), ())),
                           preferred_element_type=F32)


def _rms_scale(x):
    return lax.rsqrt(jnp.mean(x * x, axis=-1, keepdims=True) + RMS_EPS)


def _sigmoid(x):
    return 1.0 / (1.0 + jnp.exp(-x))


def _softplus(x):
    return jnp.maximum(x, 0.0) + jnp.log(1.0 + jnp.exp(-jnp.abs(x)))


def _resident(shape):
    nd = len(shape)
    return pl.BlockSpec(shape, lambda *_: (0,) * nd, pipeline_mode=pl.Buffered(1))


def _gdn_in_proj_kernel(x_ref, g_ref, wm_ref, ws_ref, om_ref, os_ref):
    x = x_ref[...]
    h = (x * _rms_scale(x) * g_ref[...]).astype(BF16)
    om_ref[...] = jnp.dot(h, wm_ref[...], preferred_element_type=F32)
    os_ref[...] = jnp.dot(h, ws_ref[...], preferred_element_type=F32)


def _gdn_in_proj(x2, gain, w_main, w_small, tm):
    n, d = x2.shape
    nm, ns = w_main.shape[1], w_small.shape[1]
    return pl.pallas_call(
        _gdn_in_proj_kernel,
        grid=(n // tm,),
        in_specs=[pl.BlockSpec((tm, d), lambda i: (i, 0)),
                  _resident((1, d)), _resident((d, nm)), _resident((d, ns))],
        out_specs=[pl.BlockSpec((tm, nm), lambda i: (i, 0)),
                   pl.BlockSpec((tm, ns), lambda i: (i, 0))],
        out_shape=[jax.ShapeDtypeStruct((n, nm), F32), jax.ShapeDtypeStruct((n, ns), F32)],
        compiler_params=_cparams("parallel"),
        name="gdn_in_proj",
    )(x2, gain, w_main, w_small)


def _fox_proj_kernel(x_ref, gq_ref, gkv_ref, wq_ref, wk_ref, wv_ref, wf_ref, bf_ref,
                     q_ref, k_ref, v_ref, lf_ref):
    x = x_ref[...]
    xn = x * _rms_scale(x)
    hq = (xn * gq_ref[...]).astype(BF16)
    hkv = (xn * gkv_ref[...]).astype(BF16)
    scale = FOX_HD ** -0.5 * LOG2_E
    q_ref[...] = (jnp.dot(hq, wq_ref[...], preferred_element_type=F32) * scale).astype(BF16)
    k_ref[...] = jnp.dot(hkv, wk_ref[...], preferred_element_type=F32).astype(BF16)
    v_ref[...] = jnp.dot(hkv, wv_ref[...], preferred_element_type=F32).astype(BF16)
    f = jnp.dot(hkv, wf_ref[...], preferred_element_type=F32) + bf_ref[...]
    lf_ref[...] = -_softplus(-f)


def _fox_proj(x2, gq, gkv, wq, wk, wv, wf, bf, tm):
    n, d = x2.shape
    w = wq.shape[1]
    row = lambda i: (i, 0)
    return pl.pallas_call(
        _fox_proj_kernel,
        grid=(n // tm,),
        in_specs=[pl.BlockSpec((tm, d), row), _resident((1, d)), _resident((1, d)),
                  _resident((d, w)), _resident((d, w)), _resident((d, w)),
                  _resident((d, LANES)), _resident((1, LANES))],
        out_specs=[pl.BlockSpec((tm, w), row), pl.BlockSpec((tm, w), row),
                   pl.BlockSpec((tm, w), row), pl.BlockSpec((tm, LANES), row)],
        out_shape=[jax.ShapeDtypeStruct((n, w), BF16), jax.ShapeDtypeStruct((n, w), BF16),
                   jax.ShapeDtypeStruct((n, w), BF16), jax.ShapeDtypeStruct((n, LANES), F32)],
        compiler_params=_cparams("parallel"),
        name="fox_proj",
    )(x2, gq, gkv, wq, wk, wv, wf, bf)


def _out_proj_kernel(a_ref, w_ref, x_ref, g_ref, o_ref):
    y = jnp.dot(a_ref[...], w_ref[...], preferred_element_type=F32)
    o_ref[...] = x_ref[...] + y * _rms_scale(y) * g_ref[...]


def _out_proj(a, w, x2, gain, tm):
    n, d = x2.shape
    k = a.shape[1]
    row = lambda i: (i, 0)
    return pl.pallas_call(
        _out_proj_kernel,
        grid=(n // tm,),
        in_specs=[pl.BlockSpec((tm, k), row), _resident((k, d)),
                  pl.BlockSpec((tm, d), row), _resident((1, d))],
        out_specs=pl.BlockSpec((tm, d), row),
        out_shape=jax.ShapeDtypeStruct((n, d), F32),
        compiler_params=_cparams("parallel"),
        name="out_proj",
    )(a, w, x2, gain)


def _ffn_kernel(x_ref, g1_ref, wg_ref, wu_ref, wo_ref, g2_ref, o_ref):
    x = x_ref[...]
    h = (x * _rms_scale(x) * g1_ref[...]).astype(BF16)
    g = jnp.dot(h, wg_ref[...], preferred_element_type=F32)
    u = jnp.dot(h, wu_ref[...], preferred_element_type=F32)
    a = (g * _sigmoid(g) * u).astype(BF16)
    f = jnp.dot(a, wo_ref[...], preferred_element_type=F32)
    o_ref[...] = x + f * _rms_scale(f) * g2_ref[...]


def _ffn(x2, g1, wg, wu, wo, g2, tm):
    n, d = x2.shape
    dff = wg.shape[1]
    row = lambda i: (i, 0)
    return pl.pallas_call(
        _ffn_kernel,
        grid=(n // tm,),
        in_specs=[pl.BlockSpec((tm, d), row), _resident((1, d)), _resident((d, dff)),
                  _resident((d, dff)), _resident((dff, d)), _resident((1, d))],
        out_specs=pl.BlockSpec((tm, d), row),
        out_shape=jax.ShapeDtypeStruct((n, d), F32),
        compiler_params=_cparams("parallel"),
        name="ffn",
    )(x2, g1, wg, wu, wo, g2)


def _gdn_kernel(qkv_ref, z_ref, gate_ref, cw_ref, alog_ref, dtb_ref, onorm_ref, o_ref,
                s_ref, tail_ref, q_s, k_s, v_s, beta_s, gcum_s, wq_s, u_s, aqk_s, kdt_s, gl_s,
                *, tb, chunks_per_prep):
    c = GDN_CHUNK
    n_chunks = tb // c
    nh = GDN_HEADS

    @pl.when(pl.program_id(1) == 0)
    def _():
        s_ref[...] = jnp.zeros_like(s_ref)
        tail_ref[...] = jnp.zeros_like(tail_ref)

    for j in range(3 * nh):
        cols = slice(j * LANES, (j + 1) * LANES)
        xx = jnp.concatenate([tail_ref[:, cols], qkv_ref[:, cols]], axis=0)
        y = None
        for kk in range(GDN_CONV):
            sh = GDN_CONV - 1 - kk
            xs = xx if sh == 0 else pltpu.roll(xx, sh, axis=0)
            term = xs[8:8 + tb] * cw_ref[kk:kk + 1, cols]
            y = term if y is None else y + term
        y = y * _sigmoid(y)
        dst = slice((j % nh) * LANES, (j % nh + 1) * LANES)
        if j < nh:
            q_s[:, dst] = y * lax.rsqrt(jnp.sum(y * y, axis=-1, keepdims=True) + RMS_EPS) * (GDN_DK ** -0.5)
        elif j < 2 * nh:
            k_s[:, dst] = y * lax.rsqrt(jnp.sum(y * y, axis=-1, keepdims=True) + RMS_EPS)
        else:
            v_s[:, dst] = y
    tail_ref[...] = qkv_ref[tb - 8:tb, :]

    gates = gate_ref[...]
    beta_s[...] = _sigmoid(gates)
    g = -jnp.exp(alog_ref[...]) * _softplus(gates + dtb_ref[...])
    row = lax.broadcasted_iota(jnp.int32, (tb, LANES), 0) & (c - 1)
    sh = 1
    while sh < c:
        g = g + jnp.where(row >= sh, pltpu.roll(g, sh, axis=0), 0.0)
        sh *= 2
    gcum_s[...] = g

    ri = lax.broadcasted_iota(jnp.int32, (c, c), 0)
    ci = lax.broadcasted_iota(jnp.int32, (c, c), 1)
    causal = ri >= ci
    strict = ri > ci
    eye = (ri == ci).astype(F32)
    level_masks = []
    s = 1
    while s < c:
        blk = ~(2 * s - 1)
        m = ((ri & blk) == (ci & blk)) & ((ri & (2 * s - 1)) >= s) & ((ci & (2 * s - 1)) < s)
        level_masks.append(m.astype(F32))
        s *= 2

    def prep_body(ip, carry):
        elems = []
        per_chunk = []
        for cc in range(chunks_per_prep):
            ic = ip * chunks_per_prep + cc
            rows = pl.ds(pl.multiple_of(ic * c, c), c)
            bt = beta_s[rows, :]
            gc = gcum_s[rows, :]
            g_last = gc[c - 1:c, :]
            per_chunk.append(dict(
                bt=bt, gc=gc, e=jnp.exp(gc), dk=jnp.exp(g_last - gc), gl=jnp.exp(g_last),
                gc_t=jnp.transpose(jnp.concatenate([gc, jnp.zeros_like(gc)], axis=0))))
            for h in range(nh):
                elems.append((cc, ic, rows, h))

        def col(name, cc, lane):
            return per_chunk[cc][name][:, lane:lane + 1]

        hcols = lambda h: slice(h * LANES, (h + 1) * LANES)
        kb = [k_s[rows, hcols(h)].astype(BF16) for (_, _, rows, h) in elems]
        kk_m = [_mm_nt(x, x) for x in kb]
        decay = []
        for (cc, _, _, h) in elems:
            diff = col("gc", cc, nh + h) - per_chunk[cc]["gc_t"][nh + h:nh + h + 1, :c]
            decay.append(jnp.where(causal, jnp.exp(jnp.where(causal, diff, 0.0)), 0.0))
        a_strict = [jnp.where(strict, col("bt", cc, h) * kk_m[i] * decay[i], 0.0)
                    for i, (cc, _, _, h) in enumerate(elems)]
        d = [eye - a * level_masks[0] for a in a_strict]
        for m in level_masks[1:]:
            x = [_mm(a * m, di) for a, di in zip(a_strict, d)]
            d = [di - _mm(di, xi) for di, xi in zip(d, x)]
        for i, (cc, ic, rows, h) in enumerate(elems):
            kh = k_s[rows, hcols(h)]
            b_col = col("bt", cc, h)
            rhs = jnp.concatenate([(b_col * col("e", cc, nh + h)) * kh,
                                   b_col * v_s[rows, hcols(h)]], axis=1)
            sol = _mm(d[i], rhs)
            qh = q_s[rows, hcols(h)]
            wq_s[ic, h, :c, :] = sol[:, :GDN_DK].astype(BF16)
            wq_s[ic, h, c:, :] = (qh * col("e", cc, nh + h)).astype(BF16)
            u_s[ic, h] = sol[:, GDN_DK:]
            aqk_s[ic, h] = (_mm_nt(qh, kb[i]) * decay[i]).astype(BF16)
            kdt_s[ic, h] = jnp.transpose(kh * col("dk", cc, nh + h)).astype(BF16)
            gl_s[ic, h] = jnp.broadcast_to(per_chunk[cc]["gl"][:, nh + h:nh + h + 1], (1, LANES))
        return carry

    lax.fori_loop(0, n_chunks // chunks_per_prep, prep_body, 0)

    def scan_body(ic, carry):
        rows = pl.ds(pl.multiple_of(ic * c, c), c)
        st = [s_ref[h] for h in range(nh)]
        wq = [_mm(wq_s[ic, h], st[h]) for h in range(nh)]
        v_new = [u_s[ic, h] - wq[h][:c] for h in range(nh)]
        v_new_b = [x.astype(BF16) for x in v_new]
        o = [wq[h][c:] + jnp.dot(aqk_s[ic, h], v_new_b[h], preferred_element_type=F32)
             for h in range(nh)]
        for h in range(nh):
            s_ref[h] = st[h] * gl_s[ic, h] + jnp.dot(kdt_s[ic, h], v_new_b[h],
                                                    preferred_element_type=F32)
        for h in range(nh):
            cols = slice(h * LANES, (h + 1) * LANES)
            zh = z_ref[rows, cols]
            on = o[h] * _rms_scale(o[h]) * onorm_ref[...]
            o_ref[rows, cols] = (on * (zh * _sigmoid(zh))).astype(o_ref.dtype)
        return carry

    lax.fori_loop(0, n_chunks, scan_body, 0)


def _gdn_core(proj_main, proj_small, conv_w, alog_pad, dtb_pad, out_norm, tb, chunks_per_prep):
    b, t, _ = proj_main.shape
    qkv_w = 3 * GDN_HEADS * GDN_DK
    v_w = GDN_HEADS * GDN_DV
    c = GDN_CHUNK
    nc = tb // c
    return pl.pallas_call(
        functools.partial(_gdn_kernel, tb=tb, chunks_per_prep=chunks_per_prep),
        grid=(b, t // tb),
        in_specs=[pl.BlockSpec((None, tb, qkv_w), lambda i, j: (i, j, 0)),
                  pl.BlockSpec((None, tb, v_w), lambda i, j: (i, j, qkv_w // v_w)),
                  pl.BlockSpec((None, tb, LANES), lambda i, j: (i, j, 0)),
                  _resident((GDN_CONV, qkv_w)), _resident((1, LANES)), _resident((1, LANES)),
                  _resident((1, GDN_DV))],
        out_specs=pl.BlockSpec((None, tb, v_w), lambda i, j: (i, j, 0)),
        out_shape=jax.ShapeDtypeStruct((b, t, v_w), BF16),
        scratch_shapes=[pltpu.VMEM((GDN_HEADS, GDN_DK, GDN_DV), F32),
                        pltpu.VMEM((8, qkv_w), F32),
                        pltpu.VMEM((tb, GDN_HEADS * GDN_DK), F32),
                        pltpu.VMEM((tb, GDN_HEADS * GDN_DK), F32),
                        pltpu.VMEM((tb, v_w), F32),
                        pltpu.VMEM((tb, LANES), F32),
                        pltpu.VMEM((tb, LANES), F32),
                        pltpu.VMEM((nc, GDN_HEADS, 2 * c, GDN_DK), BF16),
                        pltpu.VMEM((nc, GDN_HEADS, c, GDN_DV), F32),
                        pltpu.VMEM((nc, GDN_HEADS, c, c), BF16),
                        pltpu.VMEM((nc, GDN_HEADS, GDN_DK, c), BF16),
                        pltpu.VMEM((nc, GDN_HEADS, 1, LANES), F32)],
        compiler_params=_cparams("parallel", "arbitrary"),
        name="gdn_core",
    )(proj_main, proj_main, proj_small, conv_w, alog_pad, dtb_pad, out_norm)


def _cumsum_kernel(x_ref, o_ref):
    x = x_ref[...]
    t = x.shape[0]
    row = lax.broadcasted_iota(jnp.int32, x.shape, 0)
    sh = 1
    while sh < t:
        x = x + jnp.where(row >= sh, pltpu.roll(x, sh, axis=0), 0.0)
        sh *= 2
    o_ref[...] = x * LOG2_E


def _time_cumsum(lf):
    b, t, w = lf.shape
    return pl.pallas_call(
        _cumsum_kernel,
        grid=(b,),
        in_specs=[pl.BlockSpec((None, t, w), lambda i: (i, 0, 0))],
        out_specs=pl.BlockSpec((None, t, w), lambda i: (i, 0, 0)),
        out_shape=jax.ShapeDtypeStruct((b, t, w), F32),
        compiler_params=_cparams("parallel"),
        name="forget_cumsum",
    )(lf)


def _bias_terms_kernel(c_ref, selq_ref, selk_ref, oneq_ref, onek_ref, qa_ref, ka_ref):
    c = c_ref[...]
    hi = c.astype(BF16)
    r = c - hi.astype(F32)
    mid = r.astype(BF16)
    lo = (r - mid.astype(F32)).astype(BF16)
    parts = (hi, mid, lo)
    qa = oneq_ref[...]
    ka = onek_ref[...]
    for i in range(3):
        qa = qa + jnp.dot(parts[i], selq_ref[i], preferred_element_type=F32)
        ka = ka - jnp.dot(parts[i], selk_ref[i], preferred_element_type=F32)
    qa_ref[...] = qa.astype(BF16)
    ka_ref[...] = ka.astype(BF16)


def _bias_layout():
    w = FOX_HEADS // 2 * LANES
    selq = np.zeros((3, LANES, w), np.float32)
    selk = np.zeros((3, LANES, w), np.float32)
    oneq = np.zeros((1, w), np.float32)
    onek = np.zeros((1, w), np.float32)
    for h in range(FOX_HEADS):
        base = (h // 2) * LANES + BIAS_TERMS * (h % 2)
        for i in range(3):
            selq[i, h, base + i] = 1.0
            selk[i, h, base + 3 + i] = 1.0
            oneq[0, base + 3 + i] = 1.0
            onek[0, base + i] = 1.0
    return (jnp.asarray(selq, BF16), jnp.asarray(selk, BF16), jnp.asarray(oneq), jnp.asarray(onek))


def _bias_terms(c2, tm):
    n, _ = c2.shape
    w = FOX_HEADS // 2 * LANES
    selq, selk, oneq, onek = _bias_layout()
    row = lambda i: (i, 0)
    return pl.pallas_call(
        _bias_terms_kernel,
        grid=(n // tm,),
        in_specs=[pl.BlockSpec((tm, LANES), row), _resident((3, LANES, w)), _resident((3, LANES, w)),
                  _resident((1, w)), _resident((1, w))],
        out_specs=[pl.BlockSpec((tm, w), row), pl.BlockSpec((tm, w), row)],
        out_shape=[jax.ShapeDtypeStruct((n, w), BF16), jax.ShapeDtypeStruct((n, w), BF16)],
        compiler_params=_cparams("parallel"),
        name="forget_bias_terms",
    )(c2, selq, selk, oneq, onek)


def _attn_kernel(q_ref, qa_ref, k_ref, ka_ref, v_ref, o_ref, *, tq):
    qi = pl.program_id(2)
    lane = lax.broadcasted_iota(jnp.int32, (1, LANES), 1)
    q = q_ref[...]
    qa = qa_ref[...]
    zero = jnp.zeros_like(q)
    halves = []
    for p in range(2):
        head = (lane >> 6) == p
        bias = (lane >= BIAS_TERMS * p) & (lane < BIAS_TERMS * (p + 1))
        halves.append(jnp.concatenate([jnp.where(head, q, zero), jnp.where(bias, qa, zero)], axis=1))
    q2 = jnp.concatenate(halves, axis=0)
    ones = jnp.ones((tq, LANES), BF16)
    rc = ATTN_ROW_CHUNK
    ri = lax.broadcasted_iota(jnp.int32, (rc, tq), 0)
    ci = lax.broadcasted_iota(jnp.int32, (rc, tq), 1)

    def step(j, carry, masked):
        m, acc = carry
        rows = pl.ds(pl.multiple_of(j * tq, tq), tq)
        kf = jnp.concatenate([k_ref[rows, :], ka_ref[rows, :]], axis=1)
        vf = jnp.concatenate([v_ref[rows, :], ones], axis=1)
        chunks = [slice(r0, r0 + rc) for r0 in range(0, 2 * tq, rc)]
        s_all = [lax.dot_general(q2[ch], kf, (((1,), (1,)), ((), ())), preferred_element_type=F32)
                 for ch in chunks]
        m_out, acc_out = [], []
        for ch, s in zip(chunks, s_all):
            if masked:
                s = jnp.where(((ri + ch.start) & (tq - 1)) >= ci, s, NEG_BIG)
            m_new = jnp.maximum(m[ch], jnp.max(s, axis=-1, keepdims=True))
            alpha = jnp.exp2(m[ch] - m_new)
            pm = jnp.exp2(s - m_new).astype(BF16)
            acc_out.append(alpha * acc[ch] + jnp.dot(pm, vf, preferred_element_type=F32))
            m_out.append(m_new)
        return jnp.concatenate(m_out, axis=0), jnp.concatenate(acc_out, axis=0)

    init = (jnp.full((2 * tq, 1), NEG_BIG, F32), jnp.zeros((2 * tq, 2 * LANES), F32))
    carry = lax.fori_loop(0, qi, functools.partial(step, masked=False), init)
    _, acc = step(qi, carry, True)
    o = acc[:, :LANES] / acc[:, LANES:]
    o_ref[...] = jnp.where((lane >> 6) == 0, o[:tq], o[tq:]).astype(o_ref.dtype)


def _fox_attention(q, qa, k, ka, v, tq):
    b, t, w = q.shape
    n_pairs = w // LANES
    qspec = pl.BlockSpec((None, tq, LANES), lambda i, p, j: (i, j, p))
    kspec = pl.BlockSpec((None, t, LANES), lambda i, p, j: (i, 0, p))
    return pl.pallas_call(
        functools.partial(_attn_kernel, tq=tq),
        grid=(b, n_pairs, t // tq),
        in_specs=[qspec, qspec, kspec, kspec, kspec],
        out_specs=qspec,
        out_shape=jax.ShapeDtypeStruct((b, t, w), BF16),
        compiler_params=_cparams("parallel", "parallel", "arbitrary"),
        name="fox_attention",
    )(q, qa, k, ka, v)


def _pad_lanes(a):
    return jnp.pad(a, [(0, 0)] * (a.ndim - 1) + [(0, LANES - a.shape[-1])])


def kernel(x, pre_mix_norm, post_mix_norm, pre_ffn_norm, post_ffn_norm, w_ffn_in, w_ffn_out,
           gdn_w_in, gdn_conv, gdn_a_log, gdn_dt_bias, gdn_out_norm, gdn_w_out,
           kv_norm, w_kv, b_forget, fox_w_q, fox_w_o):
    b, t, d = x.shape
    n = b * t
    d_ff = w_ffn_out.shape[1]
    assert gdn_w_in.shape[0] == 1 and fox_w_q.shape[0] == 1, "one layer of each mixer"
    x2 = x.reshape(n, d)
    row = lambda a: a.reshape(1, -1).astype(F32)

    def ffn(x2, layer):
        w_in = w_ffn_in[layer].astype(BF16)
        return _ffn(x2, row(pre_ffn_norm[layer]), w_in[:, :d_ff], w_in[:, d_ff:],
                    w_ffn_out[layer].astype(BF16), row(post_ffn_norm[layer]), ROW_TILE)

    qkvz_w = 3 * GDN_HEADS * GDN_DK + GDN_HEADS * GDN_DV
    w_in = gdn_w_in[0]
    w_main = w_in[:, :qkvz_w].astype(BF16)
    w_small = _pad_lanes(w_in[:, qkvz_w:]).astype(BF16)
    proj_main, proj_small = _gdn_in_proj(x2, row(pre_mix_norm[0]), w_main, w_small, ROW_TILE)
    zeros_h = jnp.zeros((GDN_HEADS,), F32)
    alog_pad = _pad_lanes(jnp.concatenate([zeros_h, gdn_a_log[0].astype(F32)])[None, :])
    dtb_pad = _pad_lanes(jnp.concatenate([zeros_h, gdn_dt_bias[0].astype(F32)])[None, :])
    og = _gdn_core(proj_main.reshape(b, t, qkvz_w), proj_small.reshape(b, t, LANES),
                   gdn_conv[0].astype(F32), alog_pad, dtb_pad, row(gdn_out_norm[0]),
                   GDN_TIME_BLOCK, GDN_CHUNKS_PER_PREP)
    x2 = _out_proj(og.reshape(n, -1), gdn_w_out[0].astype(BF16), x2, row(post_mix_norm[0]), ROW_TILE)
    x2 = ffn(x2, 0)

    fw = FOX_HEADS * FOX_HD
    q, k, v, lf = _fox_proj(x2, row(pre_mix_norm[1]), row(kv_norm), fox_w_q[0].astype(BF16),
                            w_kv[:, :fw].astype(BF16), w_kv[:, fw:2 * fw].astype(BF16),
                            _pad_lanes(w_kv[:, 2 * fw:]).astype(BF16),
                            _pad_lanes(b_forget.astype(F32)[None, :]), ROW_TILE)
    c2 = _time_cumsum(lf.reshape(b, t, LANES))
    qa, ka = _bias_terms(c2.reshape(n, LANES), ROW_TILE)
    to3 = lambda a: a.reshape(b, t, fw)
    oa = _fox_attention(to3(q), to3(qa), to3(k), to3(ka), to3(v), ATTN_TILE)
    x2 = _out_proj(oa.reshape(n, fw), fox_w_o[0].astype(BF16), x2, row(post_mix_norm[1]), ROW_TILE)
    x2 = ffn(x2, 1)
    return x2.reshape(b, t, d)
```

```python
import functools
import math

import jax
import jax.numpy as jnp
import numpy as np
from jax import lax
from jax.experimental import pallas as pl
from jax.experimental.pallas import tpu as pltpu

F32 = jnp.float32
BF16 = jnp.bfloat16

RMS_EPS = 1e-6
LANES = 128
GDN_HEADS = 8
GDN_DK = 128
GDN_DV = 128
GDN_CONV = 4
GDN_CHUNK = 64
FOX_HEADS = 16
FOX_HD = 64
NEG_BIG = -1e30
LOG2_E = math.log2(math.e)
BIAS_TERMS = 6

ROW_TILE = 512
GDN_TIME_BLOCK = 256
GDN_CHUNKS_PER_PREP = 4
ATTN_TILE = 512
ATTN_ROW_CHUNK = 256

VMEM_LIMIT = 56 * 1024 * 1024


def _cparams(*sem):
    return pltpu.CompilerParams(dimension_semantics=sem, vmem_limit_bytes=VMEM_LIMIT)


def _mm(a, b):
    return jnp.dot(a.astype(BF16), b.astype(BF16), preferred_element_type=F32)


def _mm_nt(a, b):
    return lax.dot_general(a.astype(BF16), b.astype(BF16), (((1,), (1,)), ((), ())),
                           preferred_element_type=F32)


def _rms_scale(x):
    return lax.rsqrt(jnp.mean(x * x, axis=-1, keepdims=True) + RMS_EPS)


def _sigmoid(x):
    return 1.0 / (1.0 + jnp.exp(-x))


def _softplus(x):
    return jnp.maximum(x, 0.0) + jnp.log(1.0 + jnp.exp(-jnp.abs(x)))


def _resident(shape):
    nd = len(shape)
    return pl.BlockSpec(shape, lambda *_: (0,) * nd, pipeline_mode=pl.Buffered(1))


def _gdn_in_proj_kernel(x_ref, g_ref, wm_ref, ws_ref, om_ref, os_ref):
    x = x_ref[...]
    h = (x * _rms_scale(x) * g_ref[...]).astype(BF16)
    om_ref[...] = jnp.dot(h, wm_ref[...], preferred_element_type=F32)
    os_ref[...] = jnp.dot(h, ws_ref[...], preferred_element_type=F32)


def _gdn_in_proj(x2, gain, w_main, w_small, tm):
    n, d = x2.shape
    nm, ns = w_main.shape[1], w_small.shape[1]
    return pl.pallas_call(
        _gdn_in_proj_kernel,
        grid=(n // tm,),
        in_specs=[pl.BlockSpec((tm, d), lambda i: (i, 0)),
                  _resident((1, d)), _resident((d, nm)), _resident((d, ns))],
        out_specs=[pl.BlockSpec((tm, nm), lambda i: (i, 0)),
                   pl.BlockSpec((tm, ns), lambda i: (i, 0))],
        out_shape=[jax.ShapeDtypeStruct((n, nm), F32), jax.ShapeDtypeStruct((n, ns), F32)],
        compiler_params=_cparams("parallel"),
        name="gdn_in_proj",
    )(x2, gain, w_main, w_small)


def _fox_proj_kernel(x_ref, gq_ref, gkv_ref, wq_ref, wk_ref, wv_ref, wf_ref, bf_ref,
                     q_ref, k_ref, v_ref, lf_ref):
    x = x_ref[...]
    xn = x * _rms_scale(x)
    hq = (xn * gq_ref[...]).astype(BF16)
    hkv = (xn * gkv_ref[...]).astype(BF16)
    scale = FOX_HD ** -0.5 * LOG2_E
    q_ref[...] = (jnp.dot(hq, wq_ref[...], preferred_element_type=F32) * scale).astype(BF16)
    k_ref[...] = jnp.dot(hkv, wk_ref[...], preferred_element_type=F32).astype(BF16)
    v_ref[...] = jnp.dot(hkv, wv_ref[...], preferred_element_type=F32).astype(BF16)
    f = jnp.dot(hkv, wf_ref[...], preferred_element_type=F32) + bf_ref[...]
    lf_ref[...] = -_softplus(-f)


def _fox_proj(x2, gq, gkv, wq, wk, wv, wf, bf, tm):
    n, d = x2.shape
    w = wq.shape[1]
    row = lambda i: (i, 0)
    return pl.pallas_call(
        _fox_proj_kernel,
        grid=(n // tm,),
        in_specs=[pl.BlockSpec((tm, d), row), _resident((1, d)), _resident((1, d)),
                  _resident((d, w)), _resident((d, w)), _resident((d, w)),
                  _resident((d, LANES)), _resident((1, LANES))],
        out_specs=[pl.BlockSpec((tm, w), row), pl.BlockSpec((tm, w), row),
                   pl.BlockSpec((tm, w), row), pl.BlockSpec((tm, LANES), row)],
        out_shape=[jax.ShapeDtypeStruct((n, w), BF16), jax.ShapeDtypeStruct((n, w), BF16),
                   jax.ShapeDtypeStruct((n, w), BF16), jax.ShapeDtypeStruct((n, LANES), F32)],
        compiler_params=_cparams("parallel"),
        name="fox_proj",
    )(x2, gq, gkv, wq, wk, wv, wf, bf)


def _out_proj_kernel(a_ref, w_ref, x_ref, g_ref, o_ref):
    y = jnp.dot(a_ref[...], w_ref[...], preferred_element_type=F32)
    o_ref[...] = x_ref[...] + y * _rms_scale(y) * g_ref[...]


def _out_proj(a, w, x2, gain, tm):
    n, d = x2.shape
    k = a.shape[1]
    row = lambda i: (i, 0)
    return pl.pallas_call(
        _out_proj_kernel,
        grid=(n // tm,),
        in_specs=[pl.BlockSpec((tm, k), row), _resident((k, d)),
                  pl.BlockSpec((tm, d), row), _resident((1, d))],
        out_specs=pl.BlockSpec((tm, d), row),
        out_shape=jax.ShapeDtypeStruct((n, d), F32),
        compiler_params=_cparams("parallel"),
        name="out_proj",
    )(a, w, x2, gain)


def _ffn_kernel(x_ref, g1_ref, wg_ref, wu_ref, wo_ref, g2_ref, o_ref):
    x = x_ref[...]
    h = (x * _rms_scale(x) * g1_ref[...]).astype(BF16)
    g = jnp.dot(h, wg_ref[...], preferred_element_type=F32)
    u = jnp.dot(h, wu_ref[...], preferred_element_type=F32)
    a = (g * _sigmoid(g) * u).astype(BF16)
    f = jnp.dot(a, wo_ref[...], preferred_element_type=F32)
    o_ref[...] = x + f * _rms_scale(f) * g2_ref[...]


def _ffn(x2, g1, wg, wu, wo, g2, tm):
    n, d = x2.shape
    dff = wg.shape[1]
    row = lambda i: (i, 0)
    return pl.pallas_call(
        _ffn_kernel,
        grid=(n // tm,),
        in_specs=[pl.BlockSpec((tm, d), row), _resident((1, d)), _resident((d, dff)),
                  _resident((d, dff)), _resident((dff, d)), _resident((1, d))],
        out_specs=pl.BlockSpec((tm, d), row),
        out_shape=jax.ShapeDtypeStruct((n, d), F32),
        compiler_params=_cparams("parallel"),
        name="ffn",
    )(x2, g1, wg, wu, wo, g2)


def _gdn_kernel(qkv_ref, z_ref, gate_ref, cw_ref, alog_ref, dtb_ref, onorm_ref, o_ref,
                s_ref, tail_ref, q_s, k_s, v_s, beta_s, gcum_s, wq_s, u_s, aqk_s, kdt_s, gl_s,
                *, tb, chunks_per_prep):
    c = GDN_CHUNK
    n_chunks = tb // c
    nh = GDN_HEADS

    @pl.when(pl.program_id(1) == 0)
    def _():
        s_ref[...] = jnp.zeros_like(s_ref)
        tail_ref[...] = jnp.zeros_like(tail_ref)

    for j in range(3 * nh):
        cols = slice(j * LANES, (j + 1) * LANES)
        xx = jnp.concatenate([tail_ref[:, cols], qkv_ref[:, cols]], axis=0)
        x1 = pltpu.roll(xx, 1, axis=0)
        w = [cw_ref[kk:kk + 1, cols] for kk in range(GDN_CONV)]
        y = (xx * w[3] + x1 * w[2] + pltpu.roll(xx * w[1] + x1 * w[0], 2, axis=0))[8:8 + tb]
        y = y * _sigmoid(y)
        dst = slice((j % nh) * LANES, (j % nh + 1) * LANES)
        if j < nh:
            q_s[:, dst] = y * lax.rsqrt(jnp.sum(y * y, axis=-1, keepdims=True) + RMS_EPS) * (GDN_DK ** -0.5)
        elif j < 2 * nh:
            k_s[:, dst] = y * lax.rsqrt(jnp.sum(y * y, axis=-1, keepdims=True) + RMS_EPS)
        else:
            v_s[:, dst] = y
    tail_ref[...] = qkv_ref[tb - 8:tb, :]

    gates = gate_ref[...]
    beta_s[...] = _sigmoid(gates)
    g = -jnp.exp(alog_ref[...]) * _softplus(gates + dtb_ref[...])
    row = lax.broadcasted_iota(jnp.int32, (tb, LANES), 0) & (c - 1)
    sh = 1
    while sh < c:
        g = g + jnp.where(row >= sh, pltpu.roll(g, sh, axis=0), 0.0)
        sh *= 2
    gcum_s[...] = g

    ri = lax.broadcasted_iota(jnp.int32, (c, c), 0)
    ci = lax.broadcasted_iota(jnp.int32, (c, c), 1)
    causal = ri >= ci
    strict = ri > ci
    eye = (ri == ci).astype(F32)
    level_masks = []
    s = 1
    while s < c:
        blk = ~(2 * s - 1)
        m = ((ri & blk) == (ci & blk)) & ((ri & (2 * s - 1)) >= s) & ((ci & (2 * s - 1)) < s)
        level_masks.append(m.astype(F32))
        s *= 2

    def prep_body(ip, carry):
        elems = []
        per_chunk = []
        for cc in range(chunks_per_prep):
            ic = ip * chunks_per_prep + cc
            rows = pl.ds(pl.multiple_of(ic * c, c), c)
            bt = beta_s[rows, :]
            gc = gcum_s[rows, :]
            g_last = gc[c - 1:c, :]
            per_chunk.append(dict(
                bt=bt, gc=gc, e=jnp.exp(gc), dk=jnp.exp(g_last - gc), gl=jnp.exp(g_last),
                gc_t=jnp.transpose(jnp.concatenate([gc, jnp.zeros_like(gc)], axis=0))))
            for h in range(nh):
                elems.append((cc, ic, rows, h))

        def col(name, cc, lane):
            return per_chunk[cc][name][:, lane:lane + 1]

        hcols = lambda h: slice(h * LANES, (h + 1) * LANES)
        kb = [k_s[rows, hcols(h)].astype(BF16) for (_, _, rows, h) in elems]
        kk_m = [_mm_nt(x, x) for x in kb]
        decay = []
        for (cc, _, _, h) in elems:
            diff = col("gc", cc, nh + h) - per_chunk[cc]["gc_t"][nh + h:nh + h + 1, :c]
            decay.append(jnp.where(causal, jnp.exp(jnp.where(causal, diff, 0.0)), 0.0))
        a_strict = [jnp.where(strict, col("bt", cc, h) * kk_m[i] * decay[i], 0.0)
                    for i, (cc, _, _, h) in enumerate(elems)]
        d = [eye - a * level_masks[0] for a in a_strict]
        for m in level_masks[1:]:
            x = [_mm(a * m, di) for a, di in zip(a_strict, d)]
            d = [di - _mm(di, xi) for di, xi in zip(d, x)]
        for i, (cc, ic, rows, h) in enumerate(elems):
            kh = k_s[rows, hcols(h)]
            b_col = col("bt", cc, h)
            rhs = jnp.concatenate([(b_col * col("e", cc, nh + h)) * kh,
                                   b_col * v_s[rows, hcols(h)]], axis=1)
            sol = _mm(d[i], rhs)
            qh = q_s[rows, hcols(h)]
            wq_s[ic, h, :c, :] = sol[:, :GDN_DK].astype(BF16)
            wq_s[ic, h, c:, :] = (qh * col("e", cc, nh + h)).astype(BF16)
            u_s[ic, h] = sol[:, GDN_DK:]
            aqk_s[ic, h] = (_mm_nt(qh, kb[i]) * decay[i]).astype(BF16)
            kdt_s[ic, h] = jnp.transpose(kh * col("dk", cc, nh + h)).astype(BF16)
            gl_s[ic, h] = jnp.broadcast_to(per_chunk[cc]["gl"][:, nh + h:nh + h + 1], (1, LANES))
        return carry

    lax.fori_loop(0, n_chunks // chunks_per_prep, prep_body, 0)

    def scan_body(ic, carry):
        rows = pl.ds(pl.multiple_of(ic * c, c), c)
        st = [s_ref[h] for h in range(nh)]
        wq = [_mm(wq_s[ic, h], st[h]) for h in range(nh)]
        v_new = [u_s[ic, h] - wq[h][:c] for h in range(nh)]
        v_new_b = [x.astype(BF16) for x in v_new]
        o = [wq[h][c:] + jnp.dot(aqk_s[ic, h], v_new_b[h], preferred_element_type=F32)
             for h in range(nh)]
        for h in range(nh):
            s_ref[h] = st[h] * gl_s[ic, h] + jnp.dot(kdt_s[ic, h], v_new_b[h],
                                                    preferred_element_type=F32)
        for h in range(nh):
            cols = slice(h * LANES, (h + 1) * LANES)
            zh = z_ref[rows, cols]
            on = o[h] * _rms_scale(o[h]) * onorm_ref[...]
            o_ref[rows, cols] = (on * (zh * _sigmoid(zh))).astype(o_ref.dtype)
        return carry

    lax.fori_loop(0, n_chunks, scan_body, 0)


def _gdn_core(proj_main, proj_small, conv_w, alog_pad, dtb_pad, out_norm, tb, chunks_per_prep):
    b, t, _ = proj_main.shape
    qkv_w = 3 * GDN_HEADS * GDN_DK
    v_w = GDN_HEADS * GDN_DV
    c = GDN_CHUNK
    nc = tb // c
    return pl.pallas_call(
        functools.partial(_gdn_kernel, tb=tb, chunks_per_prep=chunks_per_prep),
        grid=(b, t // tb),
        in_specs=[pl.BlockSpec((None, tb, qkv_w), lambda i, j: (i, j, 0)),
                  pl.BlockSpec((None, tb, v_w), lambda i, j: (i, j, qkv_w // v_w)),
                  pl.BlockSpec((None, tb, LANES), lambda i, j: (i, j, 0)),
                  _resident((GDN_CONV, qkv_w)), _resident((1, LANES)), _resident((1, LANES)),
                  _resident((1, GDN_DV))],
        out_specs=pl.BlockSpec((None, tb, v_w), lambda i, j: (i, j, 0)),
        out_shape=jax.ShapeDtypeStruct((b, t, v_w), BF16),
        scratch_shapes=[pltpu.VMEM((GDN_HEADS, GDN_DK, GDN_DV), F32),
                        pltpu.VMEM((8, qkv_w), F32),
                        pltpu.VMEM((tb, GDN_HEADS * GDN_DK), F32),
                        pltpu.VMEM((tb, GDN_HEADS * GDN_DK), F32),
                        pltpu.VMEM((tb, v_w), F32),
                        pltpu.VMEM((tb, LANES), F32),
                        pltpu.VMEM((tb, LANES), F32),
                        pltpu.VMEM((nc, GDN_HEADS, 2 * c, GDN_DK), BF16),
                        pltpu.VMEM((nc, GDN_HEADS, c, GDN_DV), F32),
                        pltpu.VMEM((nc, GDN_HEADS, c, c), BF16),
                        pltpu.VMEM((nc, GDN_HEADS, GDN_DK, c), BF16),
                        pltpu.VMEM((nc, GDN_HEADS, 1, LANES), F32)],
        compiler_params=_cparams("parallel", "arbitrary"),
        name="gdn_core",
    )(proj_main, proj_main, proj_small, conv_w, alog_pad, dtb_pad, out_norm)


def _cumsum_kernel(x_ref, o_ref):
    x = x_ref[...]
    t = x.shape[0]
    row = lax.broadcasted_iota(jnp.int32, x.shape, 0)
    sh = 1
    while sh < t:
        x = x + jnp.where(row >= sh, pltpu.roll(x, sh, axis=0), 0.0)
        sh *= 2
    o_ref[...] = x * LOG2_E


def _time_cumsum(lf):
    b, t, w = lf.shape
    return pl.pallas_call(
        _cumsum_kernel,
        grid=(b,),
        in_specs=[pl.BlockSpec((None, t, w), lambda i: (i, 0, 0))],
        out_specs=pl.BlockSpec((None, t, w), lambda i: (i, 0, 0)),
        out_shape=jax.ShapeDtypeStruct((b, t, w), F32),
        compiler_params=_cparams("parallel"),
        name="forget_cumsum",
    )(lf)


def _bias_terms_kernel(c_ref, selq_ref, selk_ref, oneq_ref, onek_ref, qa_ref, ka_ref):
    c = c_ref[...]
    hi = c.astype(BF16)
    r = c - hi.astype(F32)
    mid = r.astype(BF16)
    lo = (r - mid.astype(F32)).astype(BF16)
    parts = (hi, mid, lo)
    qa = oneq_ref[...]
    ka = onek_ref[...]
    for i in range(3):
        qa = qa + jnp.dot(parts[i], selq_ref[i], preferred_element_type=F32)
        ka = ka - jnp.dot(parts[i], selk_ref[i], preferred_element_type=F32)
    qa_ref[...] = qa.astype(BF16)
    ka_ref[...] = ka.astype(BF16)


def _bias_layout():
    w = FOX_HEADS // 2 * LANES
    selq = np.zeros((3, LANES, w), np.float32)
    selk = np.zeros((3, LANES, w), np.float32)
    oneq = np.zeros((1, w), np.float32)
    onek = np.zeros((1, w), np.float32)
    for h in range(FOX_HEADS):
        base = (h // 2) * LANES + BIAS_TERMS * (h % 2)
        for i in range(3):
            selq[i, h, base + i] = 1.0
            selk[i, h, base + 3 + i] = 1.0
            oneq[0, base + 3 + i] = 1.0
            onek[0, base + i] = 1.0
    return (jnp.asarray(selq, BF16), jnp.asarray(selk, BF16), jnp.asarray(oneq), jnp.asarray(onek))


def _bias_terms(c2, tm):
    n, _ = c2.shape
    w = FOX_HEADS // 2 * LANES
    selq, selk, oneq, onek = _bias_layout()
    row = lambda i: (i, 0)
    return pl.pallas_call(
        _bias_terms_kernel,
        grid=(n // tm,),
        in_specs=[pl.BlockSpec((tm, LANES), row), _resident((3, LANES, w)), _resident((3, LANES, w)),
                  _resident((1, w)), _resident((1, w))],
        out_specs=[pl.BlockSpec((tm, w), row), pl.BlockSpec((tm, w), row)],
        out_shape=[jax.ShapeDtypeStruct((n, w), BF16), jax.ShapeDtypeStruct((n, w), BF16)],
        compiler_params=_cparams("parallel"),
        name="forget_bias_terms",
    )(c2, selq, selk, oneq, onek)


def _attn_kernel(q_ref, qa_ref, k_ref, ka_ref, v_ref, o_ref, s_buf, r_buf, m_buf, acc_buf, *, tq):
    qi = pl.program_id(2)
    lane = lax.broadcasted_iota(jnp.int32, (1, LANES), 1)
    q = q_ref[...]
    qa = qa_ref[...]
    zero = jnp.zeros_like(q)
    halves = []
    for p in range(2):
        head = (lane >> 6) == p
        bias = (lane >= BIAS_TERMS * p) & (lane < BIAS_TERMS * (p + 1))
        halves.append(jnp.concatenate([jnp.where(head, q, zero), jnp.where(bias, qa, zero)], axis=1))
    q2 = jnp.concatenate(halves, axis=0)
    ones = jnp.ones((tq, LANES), BF16)
    rc = ATTN_ROW_CHUNK
    chunks = [slice(r0, r0 + rc) for r0 in range(0, 2 * tq, rc)]
    n_tiles = tq // LANES
    ri = lax.broadcasted_iota(jnp.int32, (2 * tq, tq), 0) & (tq - 1)
    ci = lax.broadcasted_iota(jnp.int32, (2 * tq, tq), 1)

    def key_rows(j):
        return pl.ds(pl.multiple_of(j * tq, tq), tq)

    def logits(rows, ch):
        kf = jnp.concatenate([k_ref[rows, :], ka_ref[rows, :]], axis=1)
        return lax.dot_general(q2[ch], kf, (((1,), (1,)), ((), ())), preferred_element_type=F32)

    def row_max(s):
        return jnp.broadcast_to(jnp.max(s, axis=-1, keepdims=True), (s.shape[0], LANES))

    def absorb(s, r, m, acc, rows):
        vf = jnp.concatenate([v_ref[rows, :], ones], axis=1)
        m_new = jnp.maximum(m, r)
        alpha = jnp.exp2(m - m_new)
        pm = jnp.concatenate([jnp.exp2(s[:, i * LANES:(i + 1) * LANES] - m_new).astype(BF16)
                              for i in range(n_tiles)], axis=1)
        pv = jnp.dot(pm, vf, preferred_element_type=F32)
        return m_new, jnp.concatenate([alpha, alpha], axis=1) * acc + pv

    for ch in chunks:
        s0 = logits(key_rows(0), ch)
        s_buf[ch, :] = s0
        r_buf[ch, :] = row_max(s0)
    m_buf[...] = jnp.full(m_buf.shape, NEG_BIG, F32)
    acc_buf[...] = jnp.zeros(acc_buf.shape, F32)

    def step(j):
        for ch in chunks:
            m_new, acc_new = absorb(s_buf[ch, :], r_buf[ch, :], m_buf[ch, :], acc_buf[ch, :],
                                    key_rows(j))
            m_buf[ch, :] = m_new
            acc_buf[ch, :] = acc_new
            sn = logits(key_rows(j + 1), ch)
            s_buf[ch, :] = sn
            r_buf[ch, :] = row_max(sn)

    def double_step(i, carry):
        step(2 * i)
        step(2 * i + 1)
        return carry

    lax.fori_loop(0, qi >> 1, double_step, 0)

    @pl.when((qi & 1) == 1)
    def _():
        step(qi - 1)

    s_diag = jnp.where(ri >= ci, s_buf[...], NEG_BIG)
    _, acc = absorb(s_diag, row_max(s_diag), m_buf[...], acc_buf[...], key_rows(qi))
    o = acc[:, :LANES] / acc[:, LANES:]
    o_ref[...] = jnp.where((lane >> 6) == 0, o[:tq], o[tq:]).astype(o_ref.dtype)


def _fox_attention(q, qa, k, ka, v, tq):
    b, t, w = q.shape
    n_pairs = w // LANES
    qspec = pl.BlockSpec((None, tq, LANES), lambda i, p, j: (i, j, p))
    kspec = pl.BlockSpec((None, t, LANES), lambda i, p, j: (i, 0, p))
    return pl.pallas_call(
        functools.partial(_attn_kernel, tq=tq),
        grid=(b, n_pairs, t // tq),
        in_specs=[qspec, qspec, kspec, kspec, kspec],
        out_specs=qspec,
        out_shape=jax.ShapeDtypeStruct((b, t, w), BF16),
        scratch_shapes=[pltpu.VMEM((2 * tq, tq), F32),
                        pltpu.VMEM((2 * tq, LANES), F32),
                        pltpu.VMEM((2 * tq, LANES), F32),
                        pltpu.VMEM((2 * tq, 2 * LANES), F32)],
        compiler_params=_cparams("parallel", "parallel", "arbitrary"),
        name="fox_attention",
    )(q, qa, k, ka, v)


def _pad_lanes(a):
    return jnp.pad(a, [(0, 0)] * (a.ndim - 1) + [(0, LANES - a.shape[-1])])


def kernel(x, pre_mix_norm, post_mix_norm, pre_ffn_norm, post_ffn_norm, w_ffn_in, w_ffn_out,
           gdn_w_in, gdn_conv, gdn_a_log, gdn_dt_bias, gdn_out_norm, gdn_w_out,
           kv_norm, w_kv, b_forget, fox_w_q, fox_w_o):
    b, t, d = x.shape
    n = b * t
    d_ff = w_ffn_out.shape[1]
    assert gdn_w_in.shape[0] == 1 and fox_w_q.shape[0] == 1, "one layer of each mixer"
    x2 = x.reshape(n, d)
    row = lambda a: a.reshape(1, -1).astype(F32)

    def ffn(x2, layer):
        w_in = w_ffn_in[layer].astype(BF16)
        return _ffn(x2, row(pre_ffn_norm[layer]), w_in[:, :d_ff], w_in[:, d_ff:],
                    w_ffn_out[layer].astype(BF16), row(post_ffn_norm[layer]), ROW_TILE)

    qkvz_w = 3 * GDN_HEADS * GDN_DK + GDN_HEADS * GDN_DV
    w_in = gdn_w_in[0]
    w_main = w_in[:, :qkvz_w].astype(BF16)
    w_small = _pad_lanes(w_in[:, qkvz_w:]).astype(BF16)
    proj_main, proj_small = _gdn_in_proj(x2, row(pre_mix_norm[0]), w_main, w_small, ROW_TILE)
    zeros_h = jnp.zeros((GDN_HEADS,), F32)
    alog_pad = _pad_lanes(jnp.concatenate([zeros_h, gdn_a_log[0].astype(F32)])[None, :])
    dtb_pad = _pad_lanes(jnp.concatenate([zeros_h, gdn_dt_bias[0].astype(F32)])[None, :])
    og = _gdn_core(proj_main.reshape(b, t, qkvz_w), proj_small.reshape(b, t, LANES),
                   gdn_conv[0].astype(F32), alog_pad, dtb_pad, row(gdn_out_norm[0]),
                   GDN_TIME_BLOCK, GDN_CHUNKS_PER_PREP)
    x2 = _out_proj(og.reshape(n, -1), gdn_w_out[0].astype(BF16), x2, row(post_mix_norm[0]), ROW_TILE)
    x2 = ffn(x2, 0)

    fw = FOX_HEADS * FOX_HD
    q, k, v, lf = _fox_proj(x2, row(pre_mix_norm[1]), row(kv_norm), fox_w_q[0].astype(BF16),
                            w_kv[:, :fw].astype(BF16), w_kv[:, fw:2 * fw].astype(BF16),
                            _pad_lanes(w_kv[:, 2 * fw:]).astype(BF16),
                            _pad_lanes(b_forget.astype(F32)[None, :]), ROW_TILE)
    c2 = _time_cumsum(lf.reshape(b, t, LANES))
    qa, ka = _bias_terms(c2.reshape(n, LANES), ROW_TILE)
    to3 = lambda a: a.reshape(b, t, fw)
    oa = _fox_attention(to3(q), to3(qa), to3(k), to3(ka), to3(v), ATTN_TILE)
    x2 = _out_proj(oa.reshape(n, fw), fox_w_o[0].astype(BF16), x2, row(post_mix_norm[1]), ROW_TILE)
    x2 = ffn(x2, 1)
    return x2.reshape(b, t, d)
```

```python
import functools
import math

import jax
import jax.numpy as jnp
import numpy as np
from jax import lax
from jax.experimental import pallas as pl
from jax.experimental.pallas import tpu as pltpu

F32 = jnp.float32
BF16 = jnp.bfloat16

RMS_EPS = 1e-6
LANES = 128
GDN_HEADS = 8
GDN_DK = 128
GDN_DV = 128
GDN_CONV = 4
GDN_CHUNK = 64
FOX_HEADS = 16
FOX_HD = 64
NEG_BIG = -1e30
LOG2_E = math.log2(math.e)
BIAS_TERMS = 6

ROW_TILE = 512
FFN_SPLITS = 2
GDN_TIME_BLOCK = 256
GDN_CHUNKS_PER_PREP = 4
ATTN_TILE = 512
ATTN_ROW_CHUNK = 256

VMEM_LIMIT = 56 * 1024 * 1024


def _cparams(*sem):
    return pltpu.CompilerParams(dimension_semantics=sem, vmem_limit_bytes=VMEM_LIMIT)


def _mm(a, b):
    return jnp.dot(a.astype(BF16), b.astype(BF16), preferred_element_type=F32)


def _mm_nt(a, b):
    return lax.dot_general(a.astype(BF16), b.astype(BF16), (((1,), (1,)), ((), ())),
                           preferred_element_type=F32)


def _rms_scale(x):
    return lax.rsqrt(jnp.mean(x * x, axis=-1, keepdims=True) + RMS_EPS)


def _sigmoid(x):
    return 1.0 / (1.0 + jnp.exp(-x))


def _softplus(x):
    return jnp.maximum(x, 0.0) + jnp.log(1.0 + jnp.exp(-jnp.abs(x)))


def _resident(shape):
    nd = len(shape)
    return pl.BlockSpec(shape, lambda *_: (0,) * nd, pipeline_mode=pl.Buffered(1))


def _gdn_in_proj_kernel(x_ref, g_ref, wm_ref, ws_ref, om_ref, os_ref):
    x = x_ref[...]
    h = (x * _rms_scale(x) * g_ref[...]).astype(BF16)
    om_ref[...] = jnp.dot(h, wm_ref[...], preferred_element_type=F32)
    os_ref[...] = jnp.dot(h, ws_ref[...], preferred_element_type=F32)


def _gdn_in_proj(x2, gain, w_main, w_small, tm):
    n, d = x2.shape
    nm, ns = w_main.shape[1], w_small.shape[1]
    return pl.pallas_call(
        _gdn_in_proj_kernel,
        grid=(n // tm,),
        in_specs=[pl.BlockSpec((tm, d), lambda i: (i, 0)),
                  _resident((1, d)), _resident((d, nm)), _resident((d, ns))],
        out_specs=[pl.BlockSpec((tm, nm), lambda i: (i, 0)),
                   pl.BlockSpec((tm, ns), lambda i: (i, 0))],
        out_shape=[jax.ShapeDtypeStruct((n, nm), F32), jax.ShapeDtypeStruct((n, ns), F32)],
        compiler_params=_cparams("parallel"),
        name="gdn_in_proj",
    )(x2, gain, w_main, w_small)


def _fox_proj_kernel(x_ref, gq_ref, gkv_ref, wq_ref, wk_ref, wv_ref, wf_ref, bf_ref,
                     q_ref, k_ref, v_ref, lf_ref):
    x = x_ref[...]
    xn = x * _rms_scale(x)
    hq = (xn * gq_ref[...]).astype(BF16)
    hkv = (xn * gkv_ref[...]).astype(BF16)
    scale = FOX_HD ** -0.5 * LOG2_E
    q_ref[...] = (jnp.dot(hq, wq_ref[...], preferred_element_type=F32) * scale).astype(BF16)
    k_ref[...] = jnp.dot(hkv, wk_ref[...], preferred_element_type=F32).astype(BF16)
    v_ref[...] = jnp.dot(hkv, wv_ref[...], preferred_element_type=F32).astype(BF16)
    f = jnp.dot(hkv, wf_ref[...], preferred_element_type=F32) + bf_ref[...]
    lf_ref[...] = -_softplus(-f)


def _fox_proj(x2, gq, gkv, wq, wk, wv, wf, bf, tm):
    n, d = x2.shape
    w = wq.shape[1]
    row = lambda i: (i, 0)
    return pl.pallas_call(
        _fox_proj_kernel,
        grid=(n // tm,),
        in_specs=[pl.BlockSpec((tm, d), row), _resident((1, d)), _resident((1, d)),
                  _resident((d, w)), _resident((d, w)), _resident((d, w)),
                  _resident((d, LANES)), _resident((1, LANES))],
        out_specs=[pl.BlockSpec((tm, w), row), pl.BlockSpec((tm, w), row),
                   pl.BlockSpec((tm, w), row), pl.BlockSpec((tm, LANES), row)],
        out_shape=[jax.ShapeDtypeStruct((n, w), BF16), jax.ShapeDtypeStruct((n, w), BF16),
                   jax.ShapeDtypeStruct((n, w), BF16), jax.ShapeDtypeStruct((n, LANES), F32)],
        compiler_params=_cparams("parallel"),
        name="fox_proj",
    )(x2, gq, gkv, wq, wk, wv, wf, bf)


def _mix_ffn_kernel(a_ref, wmix_ref, x_ref, gmix_ref, g1_ref, wg_ref, wu_ref, wo_ref, g2_ref,
                    o_ref, *, ff_splits):
    y = jnp.dot(a_ref[...], wmix_ref[...], preferred_element_type=F32)
    x = x_ref[...] + y * _rms_scale(y) * gmix_ref[...]
    h = (x * _rms_scale(x) * g1_ref[...]).astype(BF16)
    step = wg_ref.shape[1] // ff_splits
    f = None
    for i in range(ff_splits):
        cols = slice(i * step, (i + 1) * step)
        g = jnp.dot(h, wg_ref[:, cols], preferred_element_type=F32)
        u = jnp.dot(h, wu_ref[:, cols], preferred_element_type=F32)
        act = (g * _sigmoid(g) * u).astype(BF16)
        part = jnp.dot(act, wo_ref[cols, :], preferred_element_type=F32)
        f = part if f is None else f + part
    o_ref[...] = x + f * _rms_scale(f) * g2_ref[...]


def _mix_ffn(a, wmix, x2, gmix, g1, wg, wu, wo, g2, tm, ff_splits):
    n, d = x2.shape
    k = a.shape[1]
    dff = wg.shape[1]
    assert dff % (ff_splits * LANES) == 0
    row = lambda i: (i, 0)
    return pl.pallas_call(
        functools.partial(_mix_ffn_kernel, ff_splits=ff_splits),
        grid=(n // tm,),
        in_specs=[pl.BlockSpec((tm, k), row), _resident((k, d)), pl.BlockSpec((tm, d), row),
                  _resident((1, d)), _resident((1, d)), _resident((d, dff)), _resident((d, dff)),
                  _resident((dff, d)), _resident((1, d))],
        out_specs=pl.BlockSpec((tm, d), row),
        out_shape=jax.ShapeDtypeStruct((n, d), F32),
        compiler_params=_cparams("parallel"),
        name="mix_ffn",
    )(a, wmix, x2, gmix, g1, wg, wu, wo, g2)


def _gdn_kernel(qkv_ref, z_ref, gate_ref, cw_ref, alog_ref, dtb_ref, onorm_ref, o_ref,
                s_ref, tail_ref, q_s, k_s, v_s, beta_s, gcum_s, wq_s, u_s, aqk_s, kdt_s, gl_s,
                *, tb, chunks_per_prep):
    c = GDN_CHUNK
    n_chunks = tb // c
    nh = GDN_HEADS

    @pl.when(pl.program_id(1) == 0)
    def _():
        s_ref[...] = jnp.zeros_like(s_ref)
        tail_ref[...] = jnp.zeros_like(tail_ref)

    for j in range(3 * nh):
        cols = slice(j * LANES, (j + 1) * LANES)
        xx = jnp.concatenate([tail_ref[:, cols], qkv_ref[:, cols]], axis=0)
        x1 = pltpu.roll(xx, 1, axis=0)
        w = [cw_ref[kk:kk + 1, cols] for kk in range(GDN_CONV)]
        y = (xx * w[3] + x1 * w[2] + pltpu.roll(xx * w[1] + x1 * w[0], 2, axis=0))[8:8 + tb]
        y = y * _sigmoid(y)
        dst = slice((j % nh) * LANES, (j % nh + 1) * LANES)
        if j < nh:
            q_s[:, dst] = y * lax.rsqrt(jnp.sum(y * y, axis=-1, keepdims=True) + RMS_EPS) * (GDN_DK ** -0.5)
        elif j < 2 * nh:
            k_s[:, dst] = y * lax.rsqrt(jnp.sum(y * y, axis=-1, keepdims=True) + RMS_EPS)
        else:
            v_s[:, dst] = y
    tail_ref[...] = qkv_ref[tb - 8:tb, :]

    gates = gate_ref[...]
    beta_s[...] = _sigmoid(gates)
    g = -jnp.exp(alog_ref[...]) * _softplus(gates + dtb_ref[...])
    row = lax.broadcasted_iota(jnp.int32, (tb, LANES), 0) & (c - 1)
    sh = 1
    while sh < c:
        g = g + jnp.where(row >= sh, pltpu.roll(g, sh, axis=0), 0.0)
        sh *= 2
    gcum_s[...] = g

    ri = lax.broadcasted_iota(jnp.int32, (c, c), 0)
    ci = lax.broadcasted_iota(jnp.int32, (c, c), 1)
    causal = ri >= ci
    strict = ri > ci
    eye = (ri == ci).astype(F32)
    level_masks = []
    s = 1
    while s < c:
        blk = ~(2 * s - 1)
        m = ((ri & blk) == (ci & blk)) & ((ri & (2 * s - 1)) >= s) & ((ci & (2 * s - 1)) < s)
        level_masks.append(m.astype(F32))
        s *= 2

    def prep_body(ip, carry):
        elems = []
        per_chunk = []
        for cc in range(chunks_per_prep):
            ic = ip * chunks_per_prep + cc
            rows = pl.ds(pl.multiple_of(ic * c, c), c)
            bt = beta_s[rows, :]
            gc = gcum_s[rows, :]
            g_last = gc[c - 1:c, :]
            per_chunk.append(dict(
                bt=bt, gc=gc, e=jnp.exp(gc), dk=jnp.exp(g_last - gc), gl=jnp.exp(g_last),
                gc_t=jnp.transpose(jnp.concatenate([gc, jnp.zeros_like(gc)], axis=0))))
            for h in range(nh):
                elems.append((cc, ic, rows, h))

        def col(name, cc, lane):
            return per_chunk[cc][name][:, lane:lane + 1]

        hcols = lambda h: slice(h * LANES, (h + 1) * LANES)
        kb = [k_s[rows, hcols(h)].astype(BF16) for (_, _, rows, h) in elems]
        kk_m = [_mm_nt(x, x) for x in kb]
        decay = []
        for (cc, _, _, h) in elems:
            diff = col("gc", cc, nh + h) - per_chunk[cc]["gc_t"][nh + h:nh + h + 1, :c]
            decay.append(jnp.where(causal, jnp.exp(jnp.where(causal, diff, 0.0)), 0.0))
        a_strict = [jnp.where(strict, col("bt", cc, h) * kk_m[i] * decay[i], 0.0)
                    for i, (cc, _, _, h) in enumerate(elems)]
        d = [eye - a * level_masks[0] for a in a_strict]
        for m in level_masks[1:]:
            x = [_mm(a * m, di) for a, di in zip(a_strict, d)]
            d = [di - _mm(di, xi) for di, xi in zip(d, x)]
        for i, (cc, ic, rows, h) in enumerate(elems):
            kh = k_s[rows, hcols(h)]
            b_col = col("bt", cc, h)
            rhs = jnp.concatenate([(b_col * col("e", cc, nh + h)) * kh,
                                   b_col * v_s[rows, hcols(h)]], axis=1)
            sol = _mm(d[i], rhs)
            qh = q_s[rows, hcols(h)]
            wq_s[ic, h, :c, :] = sol[:, :GDN_DK].astype(BF16)
            wq_s[ic, h, c:, :] = (qh * col("e", cc, nh + h)).astype(BF16)
            u_s[ic, h] = sol[:, GDN_DK:]
            aqk_s[ic, h] = (_mm_nt(qh, kb[i]) * decay[i]).astype(BF16)
            kdt_s[ic, h] = jnp.transpose(kh * col("dk", cc, nh + h)).astype(BF16)
            gl_s[ic, h] = jnp.broadcast_to(per_chunk[cc]["gl"][:, nh + h:nh + h + 1], (1, LANES))
        return carry

    lax.fori_loop(0, n_chunks // chunks_per_prep, prep_body, 0)

    def scan_body(ic, carry):
        rows = pl.ds(pl.multiple_of(ic * c, c), c)
        st = [s_ref[h] for h in range(nh)]
        wq = [_mm(wq_s[ic, h], st[h]) for h in range(nh)]
        v_new = [u_s[ic, h] - wq[h][:c] for h in range(nh)]
        v_new_b = [x.astype(BF16) for x in v_new]
        o = [wq[h][c:] + jnp.dot(aqk_s[ic, h], v_new_b[h], preferred_element_type=F32)
             for h in range(nh)]
        for h in range(nh):
            s_ref[h] = st[h] * gl_s[ic, h] + jnp.dot(kdt_s[ic, h], v_new_b[h],
                                                    preferred_element_type=F32)
        for h in range(nh):
            cols = slice(h * LANES, (h + 1) * LANES)
            zh = z_ref[rows, cols]
            on = o[h] * _rms_scale(o[h]) * onorm_ref[...]
            o_ref[rows, cols] = (on * (zh * _sigmoid(zh))).astype(o_ref.dtype)
        return carry

    lax.fori_loop(0, n_chunks, scan_body, 0)


def _gdn_core(proj_main, proj_small, conv_w, alog_pad, dtb_pad, out_norm, tb, chunks_per_prep):
    b, t, _ = proj_main.shape
    qkv_w = 3 * GDN_HEADS * GDN_DK
    v_w = GDN_HEADS * GDN_DV
    c = GDN_CHUNK
    nc = tb // c
    return pl.pallas_call(
        functools.partial(_gdn_kernel, tb=tb, chunks_per_prep=chunks_per_prep),
        grid=(b, t // tb),
        in_specs=[pl.BlockSpec((None, tb, qkv_w), lambda i, j: (i, j, 0)),
                  pl.BlockSpec((None, tb, v_w), lambda i, j: (i, j, qkv_w // v_w)),
                  pl.BlockSpec((None, tb, LANES), lambda i, j: (i, j, 0)),
                  _resident((GDN_CONV, qkv_w)), _resident((1, LANES)), _resident((1, LANES)),
                  _resident((1, GDN_DV))],
        out_specs=pl.BlockSpec((None, tb, v_w), lambda i, j: (i, j, 0)),
        out_shape=jax.ShapeDtypeStruct((b, t, v_w), BF16),
        scratch_shapes=[pltpu.VMEM((GDN_HEADS, GDN_DK, GDN_DV), F32),
                        pltpu.VMEM((8, qkv_w), F32),
                        pltpu.VMEM((tb, GDN_HEADS * GDN_DK), F32),
                        pltpu.VMEM((tb, GDN_HEADS * GDN_DK), F32),
                        pltpu.VMEM((tb, v_w), F32),
                        pltpu.VMEM((tb, LANES), F32),
                        pltpu.VMEM((tb, LANES), F32),
                        pltpu.VMEM((nc, GDN_HEADS, 2 * c, GDN_DK), BF16),
                        pltpu.VMEM((nc, GDN_HEADS, c, GDN_DV), F32),
                        pltpu.VMEM((nc, GDN_HEADS, c, c), BF16),
                        pltpu.VMEM((nc, GDN_HEADS, GDN_DK, c), BF16),
                        pltpu.VMEM((nc, GDN_HEADS, 1, LANES), F32)],
        compiler_params=_cparams("parallel", "arbitrary"),
        name="gdn_core",
    )(proj_main, proj_main, proj_small, conv_w, alog_pad, dtb_pad, out_norm)


def _cumsum_kernel(x_ref, o_ref):
    x = x_ref[...]
    t = x.shape[0]
    row = lax.broadcasted_iota(jnp.int32, x.shape, 0)
    sh = 1
    while sh < t:
        x = x + jnp.where(row >= sh, pltpu.roll(x, sh, axis=0), 0.0)
        sh *= 2
    o_ref[...] = x * LOG2_E


def _time_cumsum(lf):
    b, t, w = lf.shape
    return pl.pallas_call(
        _cumsum_kernel,
        grid=(b,),
        in_specs=[pl.BlockSpec((None, t, w), lambda i: (i, 0, 0))],
        out_specs=pl.BlockSpec((None, t, w), lambda i: (i, 0, 0)),
        out_shape=jax.ShapeDtypeStruct((b, t, w), F32),
        compiler_params=_cparams("parallel"),
        name="forget_cumsum",
    )(lf)


def _bias_terms_kernel(c_ref, selq_ref, selk_ref, oneq_ref, onek_ref, qa_ref, ka_ref):
    c = c_ref[...]
    hi = c.astype(BF16)
    r = c - hi.astype(F32)
    mid = r.astype(BF16)
    lo = (r - mid.astype(F32)).astype(BF16)
    parts = (hi, mid, lo)
    qa = oneq_ref[...]
    ka = onek_ref[...]
    for i in range(3):
        qa = qa + jnp.dot(parts[i], selq_ref[i], preferred_element_type=F32)
        ka = ka - jnp.dot(parts[i], selk_ref[i], preferred_element_type=F32)
    qa_ref[...] = qa.astype(BF16)
    ka_ref[...] = ka.astype(BF16)


def _bias_layout():
    w = FOX_HEADS // 2 * LANES
    selq = np.zeros((3, LANES, w), np.float32)
    selk = np.zeros((3, LANES, w), np.float32)
    oneq = np.zeros((1, w), np.float32)
    onek = np.zeros((1, w), np.float32)
    for h in range(FOX_HEADS):
        base = (h // 2) * LANES + BIAS_TERMS * (h % 2)
        for i in range(3):
            selq[i, h, base + i] = 1.0
            selk[i, h, base + 3 + i] = 1.0
            oneq[0, base + 3 + i] = 1.0
            onek[0, base + i] = 1.0
    return (jnp.asarray(selq, BF16), jnp.asarray(selk, BF16), jnp.asarray(oneq), jnp.asarray(onek))


def _bias_terms(c2, tm):
    n, _ = c2.shape
    w = FOX_HEADS // 2 * LANES
    selq, selk, oneq, onek = _bias_layout()
    row = lambda i: (i, 0)
    return pl.pallas_call(
        _bias_terms_kernel,
        grid=(n // tm,),
        in_specs=[pl.BlockSpec((tm, LANES), row), _resident((3, LANES, w)), _resident((3, LANES, w)),
                  _resident((1, w)), _resident((1, w))],
        out_specs=[pl.BlockSpec((tm, w), row), pl.BlockSpec((tm, w), row)],
        out_shape=[jax.ShapeDtypeStruct((n, w), BF16), jax.ShapeDtypeStruct((n, w), BF16)],
        compiler_params=_cparams("parallel"),
        name="forget_bias_terms",
    )(c2, selq, selk, oneq, onek)


def _attn_kernel(q_ref, qa_ref, k_ref, ka_ref, v_ref, o_ref, q2_buf, s_buf, r_buf, m_buf, acc_buf,
                 *, tq):
    n_q = q_ref.shape[0] // tq
    lane = lax.broadcasted_iota(jnp.int32, (1, LANES), 1)
    ones = jnp.ones((tq, LANES), BF16)
    rc = ATTN_ROW_CHUNK
    chunks = [slice(r0, r0 + rc) for r0 in range(0, 2 * tq, rc)]
    n_tiles = tq // LANES
    ri = lax.broadcasted_iota(jnp.int32, (rc, tq), 0)
    ci = lax.broadcasted_iota(jnp.int32, (rc, tq), 1)

    def key_rows(j):
        return pl.ds(pl.multiple_of(j * tq, tq), tq)

    def tile_rows(qi, ch):
        return pl.ds(pl.multiple_of(qi * tq + (ch.start & (tq - 1)), rc), rc)

    def stacked_q(qi, ch):
        p = ch.start // tq
        q = q_ref[tile_rows(qi, ch), :]
        qa = qa_ref[tile_rows(qi, ch), :]
        head = (lane >> 6) == p
        bias = (lane >= BIAS_TERMS * p) & (lane < BIAS_TERMS * (p + 1))
        zero = jnp.zeros_like(q)
        return jnp.concatenate([jnp.where(head, q, zero), jnp.where(bias, qa, zero)], axis=1)

    def logits(rows, ch):
        kf = jnp.concatenate([k_ref[rows, :], ka_ref[rows, :]], axis=1)
        return lax.dot_general(q2_buf[ch, :], kf, (((1,), (1,)), ((), ())),
                               preferred_element_type=F32)

    def row_max(s):
        return jnp.broadcast_to(jnp.max(s, axis=-1, keepdims=True), (s.shape[0], LANES))

    def absorb(ch, rows):
        vf = jnp.concatenate([v_ref[rows, :], ones], axis=1)
        s, m = s_buf[ch, :], m_buf[ch, :]
        m_new = jnp.maximum(m, r_buf[ch, :])
        alpha = jnp.exp2(m - m_new)
        pm = jnp.concatenate([jnp.exp2(s[:, i * LANES:(i + 1) * LANES] - m_new).astype(BF16)
                              for i in range(n_tiles)], axis=1)
        pv = jnp.dot(pm, vf, preferred_element_type=F32)
        return m_new, jnp.concatenate([alpha, alpha], axis=1) * acc_buf[ch, :] + pv

    def refill(j, ch):
        sn = logits(key_rows(j), ch)
        s_buf[ch, :] = sn
        r_buf[ch, :] = row_max(sn)

    def reset(ch):
        m_buf[ch, :] = jnp.full((rc, LANES), NEG_BIG, F32)
        acc_buf[ch, :] = jnp.zeros((rc, 2 * LANES), F32)

    for ch in chunks:
        q2_buf[ch, :] = stacked_q(0, ch)
        reset(ch)
        refill(0, ch)

    def tile(qi, carry):
        def step(j):
            for ch in chunks:
                m_new, acc_new = absorb(ch, key_rows(j))
                m_buf[ch, :] = m_new
                acc_buf[ch, :] = acc_new
                refill(j + 1, ch)

        def quad_step(i, c):
            for u in range(4):
                step(4 * i + u)
            return c

        lax.fori_loop(0, qi >> 2, quad_step, 0)
        done = qi & ~3

        @pl.when((qi & 2) != 0)
        def _():
            step(done)
            step(done + 1)

        @pl.when((qi & 1) != 0)
        def _():
            step(qi - 1)

        for ch in chunks:
            s = jnp.where(ri + (ch.start & (tq - 1)) >= ci, s_buf[ch, :], NEG_BIG)
            s_buf[ch, :] = s
            r_buf[ch, :] = row_max(s)
        q_next = jnp.minimum(qi + 1, n_q - 1)
        for ch in chunks:
            _, acc = absorb(ch, key_rows(qi))
            o = (acc[:, :LANES] / acc[:, LANES:]).astype(o_ref.dtype)
            if ch.start < tq:
                o_ref[tile_rows(qi, ch), :] = o
            else:
                o_ref[tile_rows(qi, ch), :] = jnp.where((lane >> 6) == 1, o,
                                                        o_ref[tile_rows(qi, ch), :])
            reset(ch)
            q2_buf[ch, :] = stacked_q(q_next, ch)
            refill(0, ch)
        return carry

    lax.fori_loop(0, n_q, tile, 0)


def _fox_attention(q, qa, k, ka, v, tq):
    b, t, w = q.shape
    n_pairs = w // LANES
    slab = pl.BlockSpec((None, t, LANES), lambda i, p: (i, 0, p))
    return pl.pallas_call(
        functools.partial(_attn_kernel, tq=tq),
        grid=(b, n_pairs),
        in_specs=[slab] * 5,
        out_specs=slab,
        out_shape=jax.ShapeDtypeStruct((b, t, w), BF16),
        scratch_shapes=[pltpu.VMEM((2 * tq, 2 * LANES), BF16),
                        pltpu.VMEM((2 * tq, tq), F32),
                        pltpu.VMEM((2 * tq, LANES), F32),
                        pltpu.VMEM((2 * tq, LANES), F32),
                        pltpu.VMEM((2 * tq, 2 * LANES), F32)],
        compiler_params=_cparams("parallel", "parallel"),
        name="fox_attention",
    )(q, qa, k, ka, v)


def _pad_lanes(a):
    return jnp.pad(a, [(0, 0)] * (a.ndim - 1) + [(0, LANES - a.shape[-1])])


def kernel(x, pre_mix_norm, post_mix_norm, pre_ffn_norm, post_ffn_norm, w_ffn_in, w_ffn_out,
           gdn_w_in, gdn_conv, gdn_a_log, gdn_dt_bias, gdn_out_norm, gdn_w_out,
           kv_norm, w_kv, b_forget, fox_w_q, fox_w_o):
    b, t, d = x.shape
    n = b * t
    d_ff = w_ffn_out.shape[1]
    assert gdn_w_in.shape[0] == 1 and fox_w_q.shape[0] == 1, "one layer of each mixer"
    x2 = x.reshape(n, d)
    row = lambda a: a.reshape(1, -1).astype(F32)

    def mix_ffn(mix, w_mix, x2, layer):
        w_in = w_ffn_in[layer].astype(BF16)
        return _mix_ffn(mix, w_mix.astype(BF16), x2, row(post_mix_norm[layer]),
                        row(pre_ffn_norm[layer]), w_in[:, :d_ff], w_in[:, d_ff:],
                        w_ffn_out[layer].astype(BF16), row(post_ffn_norm[layer]),
                        ROW_TILE, FFN_SPLITS)

    qkvz_w = 3 * GDN_HEADS * GDN_DK + GDN_HEADS * GDN_DV
    w_in = gdn_w_in[0]
    w_main = w_in[:, :qkvz_w].astype(BF16)
    w_small = _pad_lanes(w_in[:, qkvz_w:]).astype(BF16)
    proj_main, proj_small = _gdn_in_proj(x2, row(pre_mix_norm[0]), w_main, w_small, ROW_TILE)
    zeros_h = jnp.zeros((GDN_HEADS,), F32)
    alog_pad = _pad_lanes(jnp.concatenate([zeros_h, gdn_a_log[0].astype(F32)])[None, :])
    dtb_pad = _pad_lanes(jnp.concatenate([zeros_h, gdn_dt_bias[0].astype(F32)])[None, :])
    og = _gdn_core(proj_main.reshape(b, t, qkvz_w), proj_small.reshape(b, t, LANES),
                   gdn_conv[0].astype(F32), alog_pad, dtb_pad, row(gdn_out_norm[0]),
                   GDN_TIME_BLOCK, GDN_CHUNKS_PER_PREP)
    x2 = mix_ffn(og.reshape(n, -1), gdn_w_out[0], x2, 0)

    fw = FOX_HEADS * FOX_HD
    q, k, v, lf = _fox_proj(x2, row(pre_mix_norm[1]), row(kv_norm), fox_w_q[0].astype(BF16),
                            w_kv[:, :fw].astype(BF16), w_kv[:, fw:2 * fw].astype(BF16),
                            _pad_lanes(w_kv[:, 2 * fw:]).astype(BF16),
                            _pad_lanes(b_forget.astype(F32)[None, :]), ROW_TILE)
    c2 = _time_cumsum(lf.reshape(b, t, LANES))
    qa, ka = _bias_terms(c2.reshape(n, LANES), ROW_TILE)
    to3 = lambda a: a.reshape(b, t, fw)
    oa = _fox_attention(to3(q), to3(qa), to3(k), to3(ka), to3(v), ATTN_TILE)
    x2 = mix_ffn(oa.reshape(n, fw), fox_w_o[0], x2, 1)
    return x2.reshape(b, t, d)
```

```python
import functools
import math

import jax
import jax.numpy as jnp
import numpy as np
from jax import lax
from jax.experimental import pallas as pl
from jax.experimental.pallas import tpu as pltpu

F32 = jnp.float32
BF16 = jnp.bfloat16

RMS_EPS = 1e-6
LANES = 128
GDN_HEADS = 8
GDN_DK = 128
GDN_DV = 128
GDN_CONV = 4
GDN_CHUNK = 64
FOX_HEADS = 16
FOX_HD = 64
NEG_BIG = -1e30
LOG2_E = math.log2(math.e)
BIAS_TERMS = 6

ROW_TILE = 512
FFN_SPLITS = 1
GDN_TIME_BLOCK = 256
GDN_CHUNKS_PER_PREP = 4
ATTN_TILE = 512
ATTN_ROW_CHUNK = 256

VMEM_LIMIT = 56 * 1024 * 1024


def _cparams(*sem):
    return pltpu.CompilerParams(dimension_semantics=sem, vmem_limit_bytes=VMEM_LIMIT)


def _mm(a, b):
    return jnp.dot(a.astype(BF16), b.astype(BF16), preferred_element_type=F32)


def _mm_nt(a, b):
    return lax.dot_general(a.astype(BF16), b.astype(BF16), (((1,), (1,)), ((), ())),
                           preferred_element_type=F32)


def _rms_scale(x):
    return lax.rsqrt(jnp.mean(x * x, axis=-1, keepdims=True) + RMS_EPS)


def _sigmoid(x):
    return 1.0 / (1.0 + jnp.exp(-x))


def _softplus(x):
    return jnp.maximum(x, 0.0) + jnp.log(1.0 + jnp.exp(-jnp.abs(x)))


def _resident(shape, col_block=0):
    idx = (0,) * (len(shape) - 1) + (col_block,)
    return pl.BlockSpec(shape, lambda *_: idx, pipeline_mode=pl.Buffered(1))


def _gdn_in_proj_kernel(x_ref, g_ref, wm_ref, ws_ref, om_ref, os_ref):
    x = x_ref[...]
    h = (x * _rms_scale(x) * g_ref[...]).astype(BF16)
    om_ref[...] = jnp.dot(h, wm_ref[...], preferred_element_type=F32)
    os_ref[...] = jnp.dot(h, ws_ref[...], preferred_element_type=F32)


def _gdn_in_proj(x2, gain, w_main, nm, w_small, tm):
    n, d = x2.shape
    ns = w_small.shape[1]
    return pl.pallas_call(
        _gdn_in_proj_kernel,
        grid=(n // tm,),
        in_specs=[pl.BlockSpec((tm, d), lambda i: (i, 0)),
                  _resident((1, d)), _resident((d, nm)), _resident((d, ns))],
        out_specs=[pl.BlockSpec((tm, nm), lambda i: (i, 0)),
                   pl.BlockSpec((tm, ns), lambda i: (i, 0))],
        out_shape=[jax.ShapeDtypeStruct((n, nm), F32), jax.ShapeDtypeStruct((n, ns), F32)],
        compiler_params=_cparams("parallel"),
        name="gdn_in_proj",
    )(x2, gain, w_main, w_small)


def _fox_proj_kernel(x_ref, gq_ref, gkv_ref, wq_ref, wk_ref, wv_ref, wf_ref, bf_ref,
                     q_ref, k_ref, v_ref, lf_ref):
    x = x_ref[...]
    xn = x * _rms_scale(x)
    hq = (xn * gq_ref[...]).astype(BF16)
    hkv = (xn * gkv_ref[...]).astype(BF16)
    scale = FOX_HD ** -0.5 * LOG2_E
    q_ref[...] = (jnp.dot(hq, wq_ref[...], preferred_element_type=F32) * scale).astype(BF16)
    k_ref[...] = jnp.dot(hkv, wk_ref[...], preferred_element_type=F32).astype(BF16)
    v_ref[...] = jnp.dot(hkv, wv_ref[...], preferred_element_type=F32).astype(BF16)
    f = jnp.dot(hkv, wf_ref[...], preferred_element_type=F32) + bf_ref[...]
    lf_ref[...] = -_softplus(-f)


def _fox_proj(x2, gq, gkv, wq, wkv, wf, bf, tm):
    n, d = x2.shape
    w = wq.shape[1]
    row = lambda i: (i, 0)
    return pl.pallas_call(
        _fox_proj_kernel,
        grid=(n // tm,),
        in_specs=[pl.BlockSpec((tm, d), row), _resident((1, d)), _resident((1, d)),
                  _resident((d, w)), _resident((d, w), 0), _resident((d, w), 1),
                  _resident((d, LANES)), _resident((1, LANES))],
        out_specs=[pl.BlockSpec((tm, w), row), pl.BlockSpec((tm, w), row),
                   pl.BlockSpec((tm, w), row), pl.BlockSpec((tm, LANES), row)],
        out_shape=[jax.ShapeDtypeStruct((n, w), BF16), jax.ShapeDtypeStruct((n, w), BF16),
                   jax.ShapeDtypeStruct((n, w), BF16), jax.ShapeDtypeStruct((n, LANES), F32)],
        compiler_params=_cparams("parallel"),
        name="fox_proj",
    )(x2, gq, gkv, wq, wkv, wkv, wf, bf)


def _mix_ffn_kernel(a_ref, wmix_ref, x_ref, gmix_ref, g1_ref, wg_ref, wu_ref, wo_ref, g2_ref,
                    o_ref, *, ff_splits):
    y = jnp.dot(a_ref[...], wmix_ref[...], preferred_element_type=F32)
    x = x_ref[...] + y * _rms_scale(y) * gmix_ref[...]
    h = (x * _rms_scale(x) * g1_ref[...]).astype(BF16)
    step = wg_ref.shape[1] // ff_splits
    f = None
    for i in range(ff_splits):
        cols = slice(i * step, (i + 1) * step)
        g = jnp.dot(h, wg_ref[:, cols], preferred_element_type=F32)
        u = jnp.dot(h, wu_ref[:, cols], preferred_element_type=F32)
        act = (g * _sigmoid(g) * u).astype(BF16)
        part = jnp.dot(act, wo_ref[cols, :], preferred_element_type=F32)
        f = part if f is None else f + part
    o_ref[...] = x + f * _rms_scale(f) * g2_ref[...]


def _mix_ffn(a, wmix, x2, gmix, g1, w_in, wo, g2, tm, ff_splits):
    n, d = x2.shape
    k = a.shape[1]
    dff = wo.shape[0]
    assert dff % (ff_splits * LANES) == 0
    row = lambda i: (i, 0)
    return pl.pallas_call(
        functools.partial(_mix_ffn_kernel, ff_splits=ff_splits),
        grid=(n // tm,),
        in_specs=[pl.BlockSpec((tm, k), row), _resident((k, d)), pl.BlockSpec((tm, d), row),
                  _resident((1, d)), _resident((1, d)), _resident((d, dff), 0),
                  _resident((d, dff), 1), _resident((dff, d)), _resident((1, d))],
        out_specs=pl.BlockSpec((tm, d), row),
        out_shape=jax.ShapeDtypeStruct((n, d), F32),
        compiler_params=_cparams("parallel"),
        name="mix_ffn",
    )(a, wmix, x2, gmix, g1, w_in, w_in, wo, g2)


def _gdn_kernel(qkv_ref, z_ref, gate_ref, cw_ref, alog_ref, dtb_ref, onorm_ref, o_ref,
                s_ref, tail_ref, q_s, k_s, v_s, beta_s, gcum_s, wq_s, u_s, aqk_s, kdt_s, gl_s,
                *, tb, chunks_per_prep):
    c = GDN_CHUNK
    n_chunks = tb // c
    nh = GDN_HEADS

    @pl.when(pl.program_id(1) == 0)
    def _():
        s_ref[...] = jnp.zeros_like(s_ref)
        tail_ref[...] = jnp.zeros_like(tail_ref)

    for j in range(3 * nh):
        cols = slice(j * LANES, (j + 1) * LANES)
        xx = jnp.concatenate([tail_ref[:, cols], qkv_ref[:, cols]], axis=0)
        x1 = pltpu.roll(xx, 1, axis=0)
        w = [cw_ref[kk:kk + 1, cols] for kk in range(GDN_CONV)]
        y = (xx * w[3] + x1 * w[2] + pltpu.roll(xx * w[1] + x1 * w[0], 2, axis=0))[8:8 + tb]
        y = y * _sigmoid(y)
        dst = slice((j % nh) * LANES, (j % nh + 1) * LANES)
        if j < nh:
            q_s[:, dst] = y * lax.rsqrt(jnp.sum(y * y, axis=-1, keepdims=True) + RMS_EPS) * (GDN_DK ** -0.5)
        elif j < 2 * nh:
            k_s[:, dst] = y * lax.rsqrt(jnp.sum(y * y, axis=-1, keepdims=True) + RMS_EPS)
        else:
            v_s[:, dst] = y
    tail_ref[...] = qkv_ref[tb - 8:tb, :]

    gates = gate_ref[...]
    beta_s[...] = _sigmoid(gates)
    g = -jnp.exp(alog_ref[...]) * _softplus(gates + dtb_ref[...])
    row = lax.broadcasted_iota(jnp.int32, (tb, LANES), 0) & (c - 1)
    sh = 1
    while sh < c:
        g = g + jnp.where(row >= sh, pltpu.roll(g, sh, axis=0), 0.0)
        sh *= 2
    gcum_s[...] = g

    ri = lax.broadcasted_iota(jnp.int32, (c, c), 0)
    ci = lax.broadcasted_iota(jnp.int32, (c, c), 1)
    causal = ri >= ci
    strict = ri > ci
    eye = (ri == ci).astype(F32)
    level_masks = []
    s = 1
    while s < c:
        blk = ~(2 * s - 1)
        m = ((ri & blk) == (ci & blk)) & ((ri & (2 * s - 1)) >= s) & ((ci & (2 * s - 1)) < s)
        level_masks.append(m.astype(F32))
        s *= 2

    def prep_body(ip, carry):
        elems = []
        per_chunk = []
        for cc in range(chunks_per_prep):
            ic = ip * chunks_per_prep + cc
            rows = pl.ds(pl.multiple_of(ic * c, c), c)
            bt = beta_s[rows, :]
            gc = gcum_s[rows, :]
            g_last = gc[c - 1:c, :]
            per_chunk.append(dict(
                bt=bt, gc=gc, e=jnp.exp(gc), dk=jnp.exp(g_last - gc), gl=jnp.exp(g_last),
                gc_t=jnp.transpose(jnp.concatenate([gc, jnp.zeros_like(gc)], axis=0))))
            for h in range(nh):
                elems.append((cc, ic, rows, h))

        def col(name, cc, lane):
            return per_chunk[cc][name][:, lane:lane + 1]

        hcols = lambda h: slice(h * LANES, (h + 1) * LANES)
        kb = [k_s[rows, hcols(h)].astype(BF16) for (_, _, rows, h) in elems]
        kk_m = [_mm_nt(x, x) for x in kb]
        decay = []
        for (cc, _, _, h) in elems:
            diff = col("gc", cc, nh + h) - per_chunk[cc]["gc_t"][nh + h:nh + h + 1, :c]
            decay.append(jnp.where(causal, jnp.exp(jnp.where(causal, diff, 0.0)), 0.0))
        a_strict = [jnp.where(strict, col("bt", cc, h) * kk_m[i] * decay[i], 0.0)
                    for i, (cc, _, _, h) in enumerate(elems)]
        d = [eye - a * level_masks[0] for a in a_strict]
        for m in level_masks[1:]:
            x = [_mm(a * m, di) for a, di in zip(a_strict, d)]
            d = [di - _mm(di, xi) for di, xi in zip(d, x)]
        for i, (cc, ic, rows, h) in enumerate(elems):
            kh = k_s[rows, hcols(h)]
            b_col = col("bt", cc, h)
            rhs = jnp.concatenate([(b_col * col("e", cc, nh + h)) * kh,
                                   b_col * v_s[rows, hcols(h)]], axis=1)
            sol = _mm(d[i], rhs)
            qh = q_s[rows, hcols(h)]
            wq_s[ic, h, :c, :] = sol[:, :GDN_DK].astype(BF16)
            wq_s[ic, h, c:, :] = (qh * col("e", cc, nh + h)).astype(BF16)
            u_s[ic, h] = sol[:, GDN_DK:]
            aqk_s[ic, h] = (_mm_nt(qh, kb[i]) * decay[i]).astype(BF16)
            kdt_s[ic, h] = jnp.transpose(kh * col("dk", cc, nh + h)).astype(BF16)
            gl_s[ic, h] = jnp.broadcast_to(per_chunk[cc]["gl"][:, nh + h:nh + h + 1], (1, LANES))
        return carry

    lax.fori_loop(0, n_chunks // chunks_per_prep, prep_body, 0)

    def scan_body(ic, carry):
        rows = pl.ds(pl.multiple_of(ic * c, c), c)
        st = [s_ref[h] for h in range(nh)]
        wq = [_mm(wq_s[ic, h], st[h]) for h in range(nh)]
        v_new = [u_s[ic, h] - wq[h][:c] for h in range(nh)]
        v_new_b = [x.astype(BF16) for x in v_new]
        o = [wq[h][c:] + jnp.dot(aqk_s[ic, h], v_new_b[h], preferred_element_type=F32)
             for h in range(nh)]
        for h in range(nh):
            s_ref[h] = st[h] * gl_s[ic, h] + jnp.dot(kdt_s[ic, h], v_new_b[h],
                                                    preferred_element_type=F32)
        for h in range(nh):
            cols = slice(h * LANES, (h + 1) * LANES)
            zh = z_ref[rows, cols]
            on = o[h] * _rms_scale(o[h]) * onorm_ref[...]
            o_ref[rows, cols] = (on * (zh * _sigmoid(zh))).astype(o_ref.dtype)
        return carry

    lax.fori_loop(0, n_chunks, scan_body, 0)


def _gdn_core(proj_main, proj_small, conv_w, alog_pad, dtb_pad, out_norm, tb, chunks_per_prep):
    b, t, _ = proj_main.shape
    qkv_w = 3 * GDN_HEADS * GDN_DK
    v_w = GDN_HEADS * GDN_DV
    c = GDN_CHUNK
    nc = tb // c
    return pl.pallas_call(
        functools.partial(_gdn_kernel, tb=tb, chunks_per_prep=chunks_per_prep),
        grid=(b, t // tb),
        in_specs=[pl.BlockSpec((None, tb, qkv_w), lambda i, j: (i, j, 0)),
                  pl.BlockSpec((None, tb, v_w), lambda i, j: (i, j, qkv_w // v_w)),
                  pl.BlockSpec((None, tb, LANES), lambda i, j: (i, j, 0)),
                  _resident((GDN_CONV, qkv_w)), _resident((1, LANES)), _resident((1, LANES)),
                  _resident((1, GDN_DV))],
        out_specs=pl.BlockSpec((None, tb, v_w), lambda i, j: (i, j, 0)),
        out_shape=jax.ShapeDtypeStruct((b, t, v_w), BF16),
        scratch_shapes=[pltpu.VMEM((GDN_HEADS, GDN_DK, GDN_DV), F32),
                        pltpu.VMEM((8, qkv_w), F32),
                        pltpu.VMEM((tb, GDN_HEADS * GDN_DK), F32),
                        pltpu.VMEM((tb, GDN_HEADS * GDN_DK), F32),
                        pltpu.VMEM((tb, v_w), F32),
                        pltpu.VMEM((tb, LANES), F32),
                        pltpu.VMEM((tb, LANES), F32),
                        pltpu.VMEM((nc, GDN_HEADS, 2 * c, GDN_DK), BF16),
                        pltpu.VMEM((nc, GDN_HEADS, c, GDN_DV), F32),
                        pltpu.VMEM((nc, GDN_HEADS, c, c), BF16),
                        pltpu.VMEM((nc, GDN_HEADS, GDN_DK, c), BF16),
                        pltpu.VMEM((nc, GDN_HEADS, 1, LANES), F32)],
        compiler_params=_cparams("parallel", "arbitrary"),
        name="gdn_core",
    )(proj_main, proj_main, proj_small, conv_w, alog_pad, dtb_pad, out_norm)


def _cumsum_kernel(x_ref, o_ref):
    x = x_ref[...]
    t = x.shape[0]
    row = lax.broadcasted_iota(jnp.int32, x.shape, 0)
    sh = 1
    while sh < t:
        x = x + jnp.where(row >= sh, pltpu.roll(x, sh, axis=0), 0.0)
        sh *= 2
    o_ref[...] = x * LOG2_E


def _time_cumsum(lf):
    b, t, w = lf.shape
    return pl.pallas_call(
        _cumsum_kernel,
        grid=(b,),
        in_specs=[pl.BlockSpec((None, t, w), lambda i: (i, 0, 0))],
        out_specs=pl.BlockSpec((None, t, w), lambda i: (i, 0, 0)),
        out_shape=jax.ShapeDtypeStruct((b, t, w), F32),
        compiler_params=_cparams("parallel"),
        name="forget_cumsum",
    )(lf)


def _bias_terms_kernel(c_ref, selq_ref, selk_ref, oneq_ref, onek_ref, qa_ref, ka_ref):
    c = c_ref[...]
    hi = c.astype(BF16)
    r = c - hi.astype(F32)
    mid = r.astype(BF16)
    lo = (r - mid.astype(F32)).astype(BF16)
    parts = (hi, mid, lo)
    qa = oneq_ref[...]
    ka = onek_ref[...]
    for i in range(3):
        qa = qa + jnp.dot(parts[i], selq_ref[i], preferred_element_type=F32)
        ka = ka - jnp.dot(parts[i], selk_ref[i], preferred_element_type=F32)
    qa_ref[...] = qa.astype(BF16)
    ka_ref[...] = ka.astype(BF16)


def _bias_layout():
    w = LANES
    assert FOX_HEADS * BIAS_TERMS <= w
    selq = np.zeros((3, LANES, w), np.float32)
    selk = np.zeros((3, LANES, w), np.float32)
    oneq = np.zeros((1, w), np.float32)
    onek = np.zeros((1, w), np.float32)
    for h in range(FOX_HEADS):
        base = BIAS_TERMS * h
        for i in range(3):
            selq[i, h, base + i] = 1.0
            selk[i, h, base + 3 + i] = 1.0
            oneq[0, base + 3 + i] = 1.0
            onek[0, base + i] = 1.0
    return (jnp.asarray(selq, BF16), jnp.asarray(selk, BF16), jnp.asarray(oneq), jnp.asarray(onek))


def _bias_terms(c2, tm):
    n, w = c2.shape
    selq, selk, oneq, onek = _bias_layout()
    row = lambda i: (i, 0)
    return pl.pallas_call(
        _bias_terms_kernel,
        grid=(n // tm,),
        in_specs=[pl.BlockSpec((tm, LANES), row), _resident((3, LANES, w)), _resident((3, LANES, w)),
                  _resident((1, w)), _resident((1, w))],
        out_specs=[pl.BlockSpec((tm, w), row), pl.BlockSpec((tm, w), row)],
        out_shape=[jax.ShapeDtypeStruct((n, w), BF16), jax.ShapeDtypeStruct((n, w), BF16)],
        compiler_params=_cparams("parallel"),
        name="forget_bias_terms",
    )(c2, selq, selk, oneq, onek)


def _attn_kernel(q_ref, qa_ref, k_ref, ka_ref, v_ref, o_ref, q2_buf, s_buf, r_buf, m_buf, acc_buf,
                 *, tq):
    n_q = q_ref.shape[0] // tq
    lane = lax.broadcasted_iota(jnp.int32, (1, LANES), 1)
    ones = jnp.ones((tq, LANES), BF16)
    rc = ATTN_ROW_CHUNK
    chunks = [slice(r0, r0 + rc) for r0 in range(0, 2 * tq, rc)]
    n_tiles = tq // LANES
    ri = lax.broadcasted_iota(jnp.int32, (rc, tq), 0)
    ci = lax.broadcasted_iota(jnp.int32, (rc, tq), 1)

    def key_rows(j):
        return pl.ds(pl.multiple_of(j * tq, tq), tq)

    def tile_rows(qi, ch):
        return pl.ds(pl.multiple_of(qi * tq + (ch.start & (tq - 1)), rc), rc)

    def stacked_q(qi, ch):
        p = ch.start // tq
        q = q_ref[tile_rows(qi, ch), :]
        qa = qa_ref[tile_rows(qi, ch), :]
        head = (lane >> 6) == p
        b0 = BIAS_TERMS * (2 * pl.program_id(1) + p)
        bias = (lane >= b0) & (lane < b0 + BIAS_TERMS)
        zero = jnp.zeros_like(q)
        return jnp.concatenate([jnp.where(head, q, zero), jnp.where(bias, qa, zero)], axis=1)

    def logits(rows, ch):
        kf = jnp.concatenate([k_ref[rows, :], ka_ref[rows, :]], axis=1)
        return lax.dot_general(q2_buf[ch, :], kf, (((1,), (1,)), ((), ())),
                               preferred_element_type=F32)

    def row_max(s):
        return jnp.broadcast_to(jnp.max(s, axis=-1, keepdims=True), (s.shape[0], LANES))

    def absorb(ch, rows):
        vf = jnp.concatenate([v_ref[rows, :], ones], axis=1)
        s, m = s_buf[ch, :], m_buf[ch, :]
        m_new = jnp.maximum(m, r_buf[ch, :])
        alpha = jnp.exp2(m - m_new)
        pm = jnp.concatenate([jnp.exp2(s[:, i * LANES:(i + 1) * LANES] - m_new).astype(BF16)
                              for i in range(n_tiles)], axis=1)
        pv = jnp.dot(pm, vf, preferred_element_type=F32)
        return m_new, jnp.concatenate([alpha, alpha], axis=1) * acc_buf[ch, :] + pv

    def refill(j, ch):
        sn = logits(key_rows(j), ch)
        s_buf[ch, :] = sn
        r_buf[ch, :] = row_max(sn)

    def reset(ch):
        m_buf[ch, :] = jnp.full((rc, LANES), NEG_BIG, F32)
        acc_buf[ch, :] = jnp.zeros((rc, 2 * LANES), F32)

    for ch in chunks:
        q2_buf[ch, :] = stacked_q(0, ch)
        reset(ch)
        refill(0, ch)

    def tile(qi, carry):
        def step(j):
            for ch in chunks:
                m_new, acc_new = absorb(ch, key_rows(j))
                m_buf[ch, :] = m_new
                acc_buf[ch, :] = acc_new
                refill(j + 1, ch)

        def quad_step(i, c):
            for u in range(4):
                step(4 * i + u)
            return c

        lax.fori_loop(0, qi >> 2, quad_step, 0)
        done = qi & ~3

        @pl.when((qi & 2) != 0)
        def _():
            step(done)
            step(done + 1)

        @pl.when((qi & 1) != 0)
        def _():
            step(qi - 1)

        for ch in chunks:
            s = jnp.where(ri + (ch.start & (tq - 1)) >= ci, s_buf[ch, :], NEG_BIG)
            s_buf[ch, :] = s
            r_buf[ch, :] = row_max(s)
        q_next = jnp.minimum(qi + 1, n_q - 1)
        for ch in chunks:
            _, acc = absorb(ch, key_rows(qi))
            o = (acc[:, :LANES] / acc[:, LANES:]).astype(o_ref.dtype)
            if ch.start < tq:
                o_ref[tile_rows(qi, ch), :] = o
            else:
                o_ref[tile_rows(qi, ch), :] = jnp.where((lane >> 6) == 1, o,
                                                        o_ref[tile_rows(qi, ch), :])
            reset(ch)
            q2_buf[ch, :] = stacked_q(q_next, ch)
            refill(0, ch)
        return carry

    lax.fori_loop(0, n_q, tile, 0)


def _fox_attention(q, qa, k, ka, v, tq):
    b, t, w = q.shape
    n_pairs = w // LANES
    slab = pl.BlockSpec((None, t, LANES), lambda i, p: (i, 0, p))
    bias_slab = pl.BlockSpec((None, t, LANES), lambda i, p: (i, 0, 0))
    return pl.pallas_call(
        functools.partial(_attn_kernel, tq=tq),
        grid=(b, n_pairs),
        in_specs=[slab, bias_slab, slab, bias_slab, slab],
        out_specs=slab,
        out_shape=jax.ShapeDtypeStruct((b, t, w), BF16),
        scratch_shapes=[pltpu.VMEM((2 * tq, 2 * LANES), BF16),
                        pltpu.VMEM((2 * tq, tq), F32),
                        pltpu.VMEM((2 * tq, LANES), F32),
                        pltpu.VMEM((2 * tq, LANES), F32),
                        pltpu.VMEM((2 * tq, 2 * LANES), F32)],
        compiler_params=_cparams("parallel", "parallel"),
        name="fox_attention",
    )(q, qa, k, ka, v)


def _pad_lanes(a):
    return jnp.pad(a, [(0, 0)] * (a.ndim - 1) + [(0, LANES - a.shape[-1])])


def kernel(x, pre_mix_norm, post_mix_norm, pre_ffn_norm, post_ffn_norm, w_ffn_in, w_ffn_out,
           gdn_w_in, gdn_conv, gdn_a_log, gdn_dt_bias, gdn_out_norm, gdn_w_out,
           kv_norm, w_kv, b_forget, fox_w_q, fox_w_o):
    b, t, d = x.shape
    n = b * t
    assert gdn_w_in.shape[0] == 1 and fox_w_q.shape[0] == 1, "one layer of each mixer"
    x2 = x.reshape(n, d)
    row = lambda a: a.reshape(1, -1).astype(F32)

    def mix_ffn(mix, w_mix, x2, layer):
        return _mix_ffn(mix, w_mix.astype(BF16), x2, row(post_mix_norm[layer]),
                        row(pre_ffn_norm[layer]), w_ffn_in[layer].astype(BF16),
                        w_ffn_out[layer].astype(BF16), row(post_ffn_norm[layer]),
                        ROW_TILE, FFN_SPLITS)

    qkvz_w = 3 * GDN_HEADS * GDN_DK + GDN_HEADS * GDN_DV
    w_in = gdn_w_in[0]
    w_small = _pad_lanes(w_in[:, qkvz_w:]).astype(BF16)
    proj_main, proj_small = _gdn_in_proj(x2, row(pre_mix_norm[0]), w_in.astype(BF16), qkvz_w,
                                         w_small, ROW_TILE)
    zeros_h = jnp.zeros((GDN_HEADS,), F32)
    alog_pad = _pad_lanes(jnp.concatenate([zeros_h, gdn_a_log[0].astype(F32)])[None, :])
    dtb_pad = _pad_lanes(jnp.concatenate([zeros_h, gdn_dt_bias[0].astype(F32)])[None, :])
    og = _gdn_core(proj_main.reshape(b, t, qkvz_w), proj_small.reshape(b, t, LANES),
                   gdn_conv[0].astype(F32), alog_pad, dtb_pad, row(gdn_out_norm[0]),
                   GDN_TIME_BLOCK, GDN_CHUNKS_PER_PREP)
    x2 = mix_ffn(og.reshape(n, -1), gdn_w_out[0], x2, 0)

    fw = FOX_HEADS * FOX_HD
    q, k, v, lf = _fox_proj(x2, row(pre_mix_norm[1]), row(kv_norm), fox_w_q[0].astype(BF16),
                            w_kv.astype(BF16), _pad_lanes(w_kv[:, 2 * fw:]).astype(BF16),
                            _pad_lanes(b_forget.astype(F32)[None, :]), ROW_TILE)
    c2 = _time_cumsum(lf.reshape(b, t, LANES))
    qa, ka = _bias_terms(c2.reshape(n, LANES), ROW_TILE)
    to3 = lambda a: a.reshape(b, t, -1)
    oa = _fox_attention(to3(q), to3(qa), to3(k), to3(ka), to3(v), ATTN_TILE)
    x2 = mix_ffn(oa.reshape(n, fw), fox_w_o[0], x2, 1)
    return x2.reshape(b, t, d)
```

```python
import functools
import math

import jax
import jax.numpy as jnp
import numpy as np
from jax import lax
from jax.experimental import pallas as pl
from jax.experimental.pallas import tpu as pltpu

F32 = jnp.float32
BF16 = jnp.bfloat16

RMS_EPS = 1e-6
LANES = 128
GDN_HEADS = 8
GDN_DK = 128
GDN_DV = 128
GDN_CONV = 4
GDN_CHUNK = 64
FOX_HEADS = 16
FOX_HD = 64
NEG_BIG = -1e30
LOG2_E = math.log2(math.e)
BIAS_TERMS = 6

ROW_TILE = 512
FFN_SPLITS = 1
GDN_PROJ_SLAB = 256
GDN_CONV_ROWS = 64
GDN_TIME_BLOCK = 256
GDN_CHUNKS_PER_PREP = 4
ATTN_TILE = 512
ATTN_ROW_CHUNK = 256

VMEM_LIMIT = 56 * 1024 * 1024


def _cparams(*sem):
    return pltpu.CompilerParams(dimension_semantics=sem, vmem_limit_bytes=VMEM_LIMIT)


def _mm(a, b):
    return jnp.dot(a.astype(BF16), b.astype(BF16), preferred_element_type=F32)


def _mm_nt(a, b):
    return lax.dot_general(a.astype(BF16), b.astype(BF16), (((1,), (1,)), ((), ())),
                           preferred_element_type=F32)


def _rms_scale(x):
    return lax.rsqrt(jnp.mean(x * x, axis=-1, keepdims=True) + RMS_EPS)


def _sigmoid(x):
    return 1.0 / (1.0 + jnp.exp(-x))


def _softplus(x):
    return jnp.maximum(x, 0.0) + jnp.log(1.0 + jnp.exp(-jnp.abs(x)))


def _resident(shape, col_block=0):
    idx = (0,) * (len(shape) - 1) + (col_block,)
    return pl.BlockSpec(shape, lambda *_: idx, pipeline_mode=pl.Buffered(1))


def _gdn_in_proj_kernel(x_ref, g_ref, wm_ref, ws_ref, cw_ref, qkv_ref, z_ref, gate_ref, tail_ref,
                        *, tiles_per_seq):
    @pl.when(pl.program_id(0) % tiles_per_seq == 0)
    def _():
        tail_ref[0:8, :] = jnp.zeros((8, tail_ref.shape[1]), F32)

    x = x_ref[...]
    tm = x.shape[0]
    h = (x * _rms_scale(x) * g_ref[...]).astype(BF16)
    qk_w = GDN_HEADS * GDN_DK
    qkv_w = qkv_ref.shape[1]
    n_slabs = qkv_w // GDN_PROJ_SLAB
    z_slabs = z_ref.shape[1] // GDN_PROJ_SLAB
    z_every = n_slabs // z_slabs
    slab = lambda j: slice(j * GDN_PROJ_SLAB, (j + 1) * GDN_PROJ_SLAB)
    proj = lambda j: jnp.dot(h, wm_ref[:, slab(j)], preferred_element_type=F32)
    for j in range(n_slabs):
        cols = slab(j)
        y = proj(j)
        if j % z_every == z_every - 1:
            z_ref[:, slab(j // z_every)] = proj(n_slabs + j // z_every)
        tail_ref[8:8 + tm, cols] = y
        w = [cw_ref[kk:kk + 1, cols] for kk in range(GDN_CONV)]
        for r0 in range(0, tm, GDN_CONV_ROWS):
            xr = tail_ref[r0:r0 + 8 + GDN_CONV_ROWS, cols]
            x1 = pltpu.roll(xr, 1, axis=0)
            c = (xr * w[3] + x1 * w[2] + pltpu.roll(xr * w[1] + x1 * w[0], 2, axis=0))[8:]
            c = c * _sigmoid(c)
            if cols.start < 2 * qk_w:
                scale = GDN_DK ** -0.5 if cols.start < qk_w else 1.0
                heads = [c[:, i:i + GDN_DK] for i in range(0, GDN_PROJ_SLAB, GDN_DK)]
                c = jnp.concatenate(
                    [hd * (lax.rsqrt(jnp.sum(hd * hd, axis=-1, keepdims=True) + RMS_EPS) * scale)
                     for hd in heads], axis=1)
            qkv_ref[r0:r0 + GDN_CONV_ROWS, cols] = c
        tail_ref[0:8, cols] = tail_ref[tm:tm + 8, cols]
    gate_ref[...] = jnp.dot(h, ws_ref[...], preferred_element_type=F32)


def _gdn_in_proj(x2, gain, w_main, qkv_w, z_w, w_small, conv_w, tm, seq_len):
    n, d = x2.shape
    ns = w_small.shape[1]
    assert seq_len % tm == 0 and qkv_w % GDN_PROJ_SLAB == 0
    row = lambda i: (i, 0)
    return pl.pallas_call(
        functools.partial(_gdn_in_proj_kernel, tiles_per_seq=seq_len // tm),
        grid=(n // tm,),
        in_specs=[pl.BlockSpec((tm, d), row), _resident((1, d)), _resident((d, qkv_w + z_w)),
                  _resident((d, ns)), _resident((GDN_CONV, qkv_w))],
        out_specs=[pl.BlockSpec((tm, qkv_w), row), pl.BlockSpec((tm, z_w), row),
                   pl.BlockSpec((tm, ns), row)],
        out_shape=[jax.ShapeDtypeStruct((n, qkv_w), F32), jax.ShapeDtypeStruct((n, z_w), F32),
                   jax.ShapeDtypeStruct((n, ns), F32)],
        scratch_shapes=[pltpu.VMEM((8 + tm, qkv_w), F32)],
        compiler_params=_cparams("arbitrary"),
        name="gdn_in_proj",
    )(x2, gain, w_main, w_small, conv_w)


def _fox_proj_kernel(x_ref, gq_ref, gkv_ref, wq_ref, wk_ref, wv_ref, wf_ref, bf_ref,
                     q_ref, k_ref, v_ref, lf_ref):
    x = x_ref[...]
    xn = x * _rms_scale(x)
    hq = (xn * gq_ref[...]).astype(BF16)
    hkv = (xn * gkv_ref[...]).astype(BF16)
    scale = FOX_HD ** -0.5 * LOG2_E
    q_ref[...] = (jnp.dot(hq, wq_ref[...], preferred_element_type=F32) * scale).astype(BF16)
    k_ref[...] = jnp.dot(hkv, wk_ref[...], preferred_element_type=F32).astype(BF16)
    v_ref[...] = jnp.dot(hkv, wv_ref[...], preferred_element_type=F32).astype(BF16)
    f = jnp.dot(hkv, wf_ref[...], preferred_element_type=F32) + bf_ref[...]
    lf_ref[...] = -_softplus(-f)


def _fox_proj(x2, gq, gkv, wq, wkv, wf, bf, tm):
    n, d = x2.shape
    w = wq.shape[1]
    row = lambda i: (i, 0)
    return pl.pallas_call(
        _fox_proj_kernel,
        grid=(n // tm,),
        in_specs=[pl.BlockSpec((tm, d), row), _resident((1, d)), _resident((1, d)),
                  _resident((d, w)), _resident((d, w), 0), _resident((d, w), 1),
                  _resident((d, LANES)), _resident((1, LANES))],
        out_specs=[pl.BlockSpec((tm, w), row), pl.BlockSpec((tm, w), row),
                   pl.BlockSpec((tm, w), row), pl.BlockSpec((tm, LANES), row)],
        out_shape=[jax.ShapeDtypeStruct((n, w), BF16), jax.ShapeDtypeStruct((n, w), BF16),
                   jax.ShapeDtypeStruct((n, w), BF16), jax.ShapeDtypeStruct((n, LANES), F32)],
        compiler_params=_cparams("parallel"),
        name="fox_proj",
    )(x2, gq, gkv, wq, wkv, wkv, wf, bf)


def _mix_ffn_kernel(a_ref, wmix_ref, x_ref, gmix_ref, g1_ref, wg_ref, wu_ref, wo_ref, g2_ref,
                    o_ref, *, ff_splits):
    y = jnp.dot(a_ref[...], wmix_ref[...], preferred_element_type=F32)
    x = x_ref[...] + y * _rms_scale(y) * gmix_ref[...]
    h = (x * _rms_scale(x) * g1_ref[...]).astype(BF16)
    step = wg_ref.shape[1] // ff_splits
    f = None
    for i in range(ff_splits):
        cols = slice(i * step, (i + 1) * step)
        g = jnp.dot(h, wg_ref[:, cols], preferred_element_type=F32)
        u = jnp.dot(h, wu_ref[:, cols], preferred_element_type=F32)
        act = (g * _sigmoid(g) * u).astype(BF16)
        part = jnp.dot(act, wo_ref[cols, :], preferred_element_type=F32)
        f = part if f is None else f + part
    o_ref[...] = x + f * _rms_scale(f) * g2_ref[...]


def _mix_ffn(a, wmix, x2, gmix, g1, w_in, wo, g2, tm, ff_splits):
    n, d = x2.shape
    k = a.shape[1]
    dff = wo.shape[0]
    assert dff % (ff_splits * LANES) == 0
    row = lambda i: (i, 0)
    return pl.pallas_call(
        functools.partial(_mix_ffn_kernel, ff_splits=ff_splits),
        grid=(n // tm,),
        in_specs=[pl.BlockSpec((tm, k), row), _resident((k, d)), pl.BlockSpec((tm, d), row),
                  _resident((1, d)), _resident((1, d)), _resident((d, dff), 0),
                  _resident((d, dff), 1), _resident((dff, d)), _resident((1, d))],
        out_specs=pl.BlockSpec((tm, d), row),
        out_shape=jax.ShapeDtypeStruct((n, d), F32),
        compiler_params=_cparams("parallel"),
        name="mix_ffn",
    )(a, wmix, x2, gmix, g1, w_in, w_in, wo, g2)


def _gdn_kernel(qkv_ref, z_ref, gate_ref, alog_ref, dtb_ref, onorm_ref, o_ref,
                s_ref, beta_s, gcum_s, wq_s, u_s, aqk_s, kdt_s, gl_s, *, tb, chunks_per_prep):
    c = GDN_CHUNK
    n_chunks = tb // c
    nh = GDN_HEADS
    k_off = nh * GDN_DK
    v_off = 2 * nh * GDN_DK

    @pl.when(pl.program_id(1) == 0)
    def _():
        s_ref[...] = jnp.zeros_like(s_ref)

    gates = gate_ref[...]
    beta_s[...] = _sigmoid(gates)
    g = -jnp.exp(alog_ref[...]) * _softplus(gates + dtb_ref[...])
    row = lax.broadcasted_iota(jnp.int32, (tb, LANES), 0) & (c - 1)
    sh = 1
    while sh < c:
        g = g + jnp.where(row >= sh, pltpu.roll(g, sh, axis=0), 0.0)
        sh *= 2
    gcum_s[...] = g

    ri = lax.broadcasted_iota(jnp.int32, (c, c), 0)
    ci = lax.broadcasted_iota(jnp.int32, (c, c), 1)
    causal = ri >= ci
    strict = ri > ci
    eye = (ri == ci).astype(F32)
    level_masks = []
    s = 1
    while s < c:
        blk = ~(2 * s - 1)
        m = ((ri & blk) == (ci & blk)) & ((ri & (2 * s - 1)) >= s) & ((ci & (2 * s - 1)) < s)
        level_masks.append(m.astype(F32))
        s *= 2

    def prep_body(ip, carry):
        elems = []
        per_chunk = []
        for cc in range(chunks_per_prep):
            ic = ip * chunks_per_prep + cc
            rows = pl.ds(pl.multiple_of(ic * c, c), c)
            bt = beta_s[rows, :]
            gc = gcum_s[rows, :]
            g_last = gc[c - 1:c, :]
            per_chunk.append(dict(
                bt=bt, gc=gc, e=jnp.exp(gc), dk=jnp.exp(g_last - gc), gl=jnp.exp(g_last),
                gc_t=jnp.transpose(jnp.concatenate([gc, jnp.zeros_like(gc)], axis=0))))
            for h in range(nh):
                elems.append((cc, ic, rows, h))

        def col(name, cc, lane):
            return per_chunk[cc][name][:, lane:lane + 1]

        hcols = lambda h, off=0: slice(off + h * LANES, off + (h + 1) * LANES)
        kb = [qkv_ref[rows, hcols(h, k_off)].astype(BF16) for (_, _, rows, h) in elems]
        kk_m = [_mm_nt(x, x) for x in kb]
        decay = []
        for (cc, _, _, h) in elems:
            diff = col("gc", cc, nh + h) - per_chunk[cc]["gc_t"][nh + h:nh + h + 1, :c]
            decay.append(jnp.where(causal, jnp.exp(jnp.where(causal, diff, 0.0)), 0.0))
        a_strict = [jnp.where(strict, col("bt", cc, h) * kk_m[i] * decay[i], 0.0)
                    for i, (cc, _, _, h) in enumerate(elems)]
        d = [eye - a * level_masks[0] for a in a_strict]
        for m in level_masks[1:]:
            x = [_mm(a * m, di) for a, di in zip(a_strict, d)]
            d = [di - _mm(di, xi) for di, xi in zip(d, x)]
        for i, (cc, ic, rows, h) in enumerate(elems):
            kh = qkv_ref[rows, hcols(h, k_off)]
            b_col = col("bt", cc, h)
            rhs = jnp.concatenate([(b_col * col("e", cc, nh + h)) * kh,
                                   b_col * qkv_ref[rows, hcols(h, v_off)]], axis=1)
            sol = _mm(d[i], rhs)
            qh = qkv_ref[rows, hcols(h)]
            wq_s[ic, h, :c, :] = sol[:, :GDN_DK].astype(BF16)
            wq_s[ic, h, c:, :] = (qh * col("e", cc, nh + h)).astype(BF16)
            u_s[ic, h] = sol[:, GDN_DK:]
            aqk_s[ic, h] = (_mm_nt(qh, kb[i]) * decay[i]).astype(BF16)
            kdt_s[ic, h] = jnp.transpose(kh * col("dk", cc, nh + h)).astype(BF16)
            gl_s[ic, h] = jnp.broadcast_to(per_chunk[cc]["gl"][:, nh + h:nh + h + 1], (1, LANES))
        return carry

    lax.fori_loop(0, n_chunks // chunks_per_prep, prep_body, 0)

    def scan_body(ic, carry):
        rows = pl.ds(pl.multiple_of(ic * c, c), c)
        st = [s_ref[h] for h in range(nh)]
        wq = [_mm(wq_s[ic, h], st[h]) for h in range(nh)]
        v_new = [u_s[ic, h] - wq[h][:c] for h in range(nh)]
        v_new_b = [x.astype(BF16) for x in v_new]
        o = [wq[h][c:] + jnp.dot(aqk_s[ic, h], v_new_b[h], preferred_element_type=F32)
             for h in range(nh)]
        for h in range(nh):
            s_ref[h] = st[h] * gl_s[ic, h] + jnp.dot(kdt_s[ic, h], v_new_b[h],
                                                    preferred_element_type=F32)
        for h in range(nh):
            cols = slice(h * LANES, (h + 1) * LANES)
            zh = z_ref[rows, cols]
            on = o[h] * _rms_scale(o[h]) * onorm_ref[...]
            o_ref[rows, cols] = (on * (zh * _sigmoid(zh))).astype(o_ref.dtype)
        return carry

    lax.fori_loop(0, n_chunks, scan_body, 0)


def _gdn_core(qkv, z, gates, alog_pad, dtb_pad, out_norm, tb, chunks_per_prep):
    b, t, qkv_w = qkv.shape
    v_w = z.shape[2]
    c = GDN_CHUNK
    nc = tb // c
    blk = lambda w: pl.BlockSpec((None, tb, w), lambda i, j: (i, j, 0))
    return pl.pallas_call(
        functools.partial(_gdn_kernel, tb=tb, chunks_per_prep=chunks_per_prep),
        grid=(b, t // tb),
        in_specs=[blk(qkv_w), blk(v_w), blk(LANES),
                  _resident((1, LANES)), _resident((1, LANES)), _resident((1, GDN_DV))],
        out_specs=blk(v_w),
        out_shape=jax.ShapeDtypeStruct((b, t, v_w), BF16),
        scratch_shapes=[pltpu.VMEM((GDN_HEADS, GDN_DK, GDN_DV), F32),
                        pltpu.VMEM((tb, LANES), F32),
                        pltpu.VMEM((tb, LANES), F32),
                        pltpu.VMEM((nc, GDN_HEADS, 2 * c, GDN_DK), BF16),
                        pltpu.VMEM((nc, GDN_HEADS, c, GDN_DV), F32),
                        pltpu.VMEM((nc, GDN_HEADS, c, c), BF16),
                        pltpu.VMEM((nc, GDN_HEADS, GDN_DK, c), BF16),
                        pltpu.VMEM((nc, GDN_HEADS, 1, LANES), F32)],
        compiler_params=_cparams("parallel", "arbitrary"),
        name="gdn_core",
    )(qkv, z, gates, alog_pad, dtb_pad, out_norm)


def _cumsum_kernel(x_ref, o_ref):
    x = x_ref[...]
    t = x.shape[0]
    row = lax.broadcasted_iota(jnp.int32, x.shape, 0)
    sh = 1
    while sh < t:
        x = x + jnp.where(row >= sh, pltpu.roll(x, sh, axis=0), 0.0)
        sh *= 2
    o_ref[...] = x * LOG2_E


def _time_cumsum(lf):
    b, t, w = lf.shape
    return pl.pallas_call(
        _cumsum_kernel,
        grid=(b,),
        in_specs=[pl.BlockSpec((None, t, w), lambda i: (i, 0, 0))],
        out_specs=pl.BlockSpec((None, t, w), lambda i: (i, 0, 0)),
        out_shape=jax.ShapeDtypeStruct((b, t, w), F32),
        compiler_params=_cparams("parallel"),
        name="forget_cumsum",
    )(lf)


def _bias_terms_kernel(c_ref, selq_ref, selk_ref, oneq_ref, onek_ref, qa_ref, ka_ref):
    c = c_ref[...]
    hi = c.astype(BF16)
    r = c - hi.astype(F32)
    mid = r.astype(BF16)
    lo = (r - mid.astype(F32)).astype(BF16)
    parts = (hi, mid, lo)
    qa = oneq_ref[...]
    ka = onek_ref[...]
    for i in range(3):
        qa = qa + jnp.dot(parts[i], selq_ref[i], preferred_element_type=F32)
        ka = ka - jnp.dot(parts[i], selk_ref[i], preferred_element_type=F32)
    qa_ref[...] = qa.astype(BF16)
    ka_ref[...] = ka.astype(BF16)


def _bias_layout():
    w = LANES
    assert FOX_HEADS * BIAS_TERMS <= w
    selq = np.zeros((3, LANES, w), np.float32)
    selk = np.zeros((3, LANES, w), np.float32)
    oneq = np.zeros((1, w), np.float32)
    onek = np.zeros((1, w), np.float32)
    for h in range(FOX_HEADS):
        base = BIAS_TERMS * h
        for i in range(3):
            selq[i, h, base + i] = 1.0
            selk[i, h, base + 3 + i] = 1.0
            oneq[0, base + 3 + i] = 1.0
            onek[0, base + i] = 1.0
    return (jnp.asarray(selq, BF16), jnp.asarray(selk, BF16), jnp.asarray(oneq), jnp.asarray(onek))


def _bias_terms(c2, tm):
    n, w = c2.shape
    selq, selk, oneq, onek = _bias_layout()
    row = lambda i: (i, 0)
    return pl.pallas_call(
        _bias_terms_kernel,
        grid=(n // tm,),
        in_specs=[pl.BlockSpec((tm, LANES), row), _resident((3, LANES, w)), _resident((3, LANES, w)),
                  _resident((1, w)), _resident((1, w))],
        out_specs=[pl.BlockSpec((tm, w), row), pl.BlockSpec((tm, w), row)],
        out_shape=[jax.ShapeDtypeStruct((n, w), BF16), jax.ShapeDtypeStruct((n, w), BF16)],
        compiler_params=_cparams("parallel"),
        name="forget_bias_terms",
    )(c2, selq, selk, oneq, onek)


def _attn_kernel(q_ref, qa_ref, k_ref, ka_ref, v_ref, o_ref, q2_buf, s_buf, r_buf, m_buf, acc_buf,
                 *, tq):
    n_q = q_ref.shape[0] // tq
    lane = lax.broadcasted_iota(jnp.int32, (1, LANES), 1)
    ones = jnp.ones((tq, LANES), BF16)
    rc = ATTN_ROW_CHUNK
    chunks = [slice(r0, r0 + rc) for r0 in range(0, 2 * tq, rc)]
    n_tiles = tq // LANES
    ri = lax.broadcasted_iota(jnp.int32, (rc, tq), 0)
    ci = lax.broadcasted_iota(jnp.int32, (rc, tq), 1)

    def key_rows(j):
        return pl.ds(pl.multiple_of(j * tq, tq), tq)

    def tile_rows(qi, ch):
        return pl.ds(pl.multiple_of(qi * tq + (ch.start & (tq - 1)), rc), rc)

    def stacked_q(qi, ch):
        p = ch.start // tq
        q = q_ref[tile_rows(qi, ch), :]
        qa = qa_ref[tile_rows(qi, ch), :]
        head = (lane >> 6) == p
        b0 = BIAS_TERMS * (2 * pl.program_id(1) + p)
        bias = (lane >= b0) & (lane < b0 + BIAS_TERMS)
        zero = jnp.zeros_like(q)
        return jnp.concatenate([jnp.where(head, q, zero), jnp.where(bias, qa, zero)], axis=1)

    def logits(rows, ch):
        kf = jnp.concatenate([k_ref[rows, :], ka_ref[rows, :]], axis=1)
        return lax.dot_general(q2_buf[ch, :], kf, (((1,), (1,)), ((), ())),
                               preferred_element_type=F32)

    def row_max(s):
        return jnp.broadcast_to(jnp.max(s, axis=-1, keepdims=True), (s.shape[0], LANES))

    def absorb(ch, rows):
        vf = jnp.concatenate([v_ref[rows, :], ones], axis=1)
        s, m = s_buf[ch, :], m_buf[ch, :]
        m_new = jnp.maximum(m, r_buf[ch, :])
        alpha = jnp.exp2(m - m_new)
        pm = jnp.concatenate([jnp.exp2(s[:, i * LANES:(i + 1) * LANES] - m_new).astype(BF16)
                              for i in range(n_tiles)], axis=1)
        pv = jnp.dot(pm, vf, preferred_element_type=F32)
        return m_new, jnp.concatenate([alpha, alpha], axis=1) * acc_buf[ch, :] + pv

    def refill(j, ch):
        sn = logits(key_rows(j), ch)
        s_buf[ch, :] = sn
        r_buf[ch, :] = row_max(sn)

    def reset(ch):
        m_buf[ch, :] = jnp.full((rc, LANES), NEG_BIG, F32)
        acc_buf[ch, :] = jnp.zeros((rc, 2 * LANES), F32)

    for ch in chunks:
        q2_buf[ch, :] = stacked_q(0, ch)
        reset(ch)
        refill(0, ch)

    def tile(qi, carry):
        def step(j):
            for ch in chunks:
                m_new, acc_new = absorb(ch, key_rows(j))
                m_buf[ch, :] = m_new
                acc_buf[ch, :] = acc_new
                refill(j + 1, ch)

        def quad_step(i, c):
            for u in range(4):
                step(4 * i + u)
            return c

        lax.fori_loop(0, qi >> 2, quad_step, 0)
        done = qi & ~3

        @pl.when((qi & 2) != 0)
        def _():
            step(done)
            step(done + 1)

        @pl.when((qi & 1) != 0)
        def _():
            step(qi - 1)

        for ch in chunks:
            s = jnp.where(ri + (ch.start & (tq - 1)) >= ci, s_buf[ch, :], NEG_BIG)
            s_buf[ch, :] = s
            r_buf[ch, :] = row_max(s)
        q_next = jnp.minimum(qi + 1, n_q - 1)
        for ch in chunks:
            _, acc = absorb(ch, key_rows(qi))
            o = (acc[:, :LANES] / acc[:, LANES:]).astype(o_ref.dtype)
            if ch.start < tq:
                o_ref[tile_rows(qi, ch), :] = o
            else:
                o_ref[tile_rows(qi, ch), :] = jnp.where((lane >> 6) == 1, o,
                                                        o_ref[tile_rows(qi, ch), :])
            reset(ch)
            q2_buf[ch, :] = stacked_q(q_next, ch)
            refill(0, ch)
        return carry

    lax.fori_loop(0, n_q, tile, 0)


def _fox_attention(q, qa, k, ka, v, tq):
    b, t, w = q.shape
    n_pairs = w // LANES
    slab = pl.BlockSpec((None, t, LANES), lambda i, p: (i, 0, p))
    bias_slab = pl.BlockSpec((None, t, LANES), lambda i, p: (i, 0, 0))
    return pl.pallas_call(
        functools.partial(_attn_kernel, tq=tq),
        grid=(b, n_pairs),
        in_specs=[slab, bias_slab, slab, bias_slab, slab],
        out_specs=slab,
        out_shape=jax.ShapeDtypeStruct((b, t, w), BF16),
        scratch_shapes=[pltpu.VMEM((2 * tq, 2 * LANES), BF16),
                        pltpu.VMEM((2 * tq, tq), F32),
                        pltpu.VMEM((2 * tq, LANES), F32),
                        pltpu.VMEM((2 * tq, LANES), F32),
                        pltpu.VMEM((2 * tq, 2 * LANES), F32)],
        compiler_params=_cparams("parallel", "parallel"),
        name="fox_attention",
    )(q, qa, k, ka, v)


def _pad_lanes(a):
    return jnp.pad(a, [(0, 0)] * (a.ndim - 1) + [(0, LANES - a.shape[-1])])


def kernel(x, pre_mix_norm, post_mix_norm, pre_ffn_norm, post_ffn_norm, w_ffn_in, w_ffn_out,
           gdn_w_in, gdn_conv, gdn_a_log, gdn_dt_bias, gdn_out_norm, gdn_w_out,
           kv_norm, w_kv, b_forget, fox_w_q, fox_w_o):
    b, t, d = x.shape
    n = b * t
    assert gdn_w_in.shape[0] == 1 and fox_w_q.shape[0] == 1, "one layer of each mixer"
    x2 = x.reshape(n, d)
    row = lambda a: a.reshape(1, -1).astype(F32)

    def mix_ffn(mix, w_mix, x2, layer):
        return _mix_ffn(mix, w_mix.astype(BF16), x2, row(post_mix_norm[layer]),
                        row(pre_ffn_norm[layer]), w_ffn_in[layer].astype(BF16),
                        w_ffn_out[layer].astype(BF16), row(post_ffn_norm[layer]),
                        ROW_TILE, FFN_SPLITS)

    qkv_w = 3 * GDN_HEADS * GDN_DK
    z_w = GDN_HEADS * GDN_DV
    w_in = gdn_w_in[0]
    w_small = _pad_lanes(w_in[:, qkv_w + z_w:]).astype(BF16)
    qkv, z, gates = _gdn_in_proj(x2, row(pre_mix_norm[0]), w_in.astype(BF16), qkv_w, z_w, w_small,
                                 gdn_conv[0].astype(F32), ROW_TILE, t)
    zeros_h = jnp.zeros((GDN_HEADS,), F32)
    alog_pad = _pad_lanes(jnp.concatenate([zeros_h, gdn_a_log[0].astype(F32)])[None, :])
    dtb_pad = _pad_lanes(jnp.concatenate([zeros_h, gdn_dt_bias[0].astype(F32)])[None, :])
    og = _gdn_core(qkv.reshape(b, t, qkv_w), z.reshape(b, t, z_w), gates.reshape(b, t, LANES),
                   alog_pad, dtb_pad, row(gdn_out_norm[0]), GDN_TIME_BLOCK, GDN_CHUNKS_PER_PREP)
    x2 = mix_ffn(og.reshape(n, -1), gdn_w_out[0], x2, 0)

    fw = FOX_HEADS * FOX_HD
    q, k, v, lf = _fox_proj(x2, row(pre_mix_norm[1]), row(kv_norm), fox_w_q[0].astype(BF16),
                            w_kv.astype(BF16), _pad_lanes(w_kv[:, 2 * fw:]).astype(BF16),
                            _pad_lanes(b_forget.astype(F32)[None, :]), ROW_TILE)
    c2 = _time_cumsum(lf.reshape(b, t, LANES))
    qa, ka = _bias_terms(c2.reshape(n, LANES), ROW_TILE)
    to3 = lambda a: a.reshape(b, t, -1)
    oa = _fox_attention(to3(q), to3(qa), to3(k), to3(ka), to3(v), ATTN_TILE)
    x2 = mix_ffn(oa.reshape(n, fw), fox_w_o[0], x2, 1)
    return x2.reshape(b, t, d)
```

```python
import functools
import math

import jax
import jax.numpy as jnp
import numpy as np
from jax import lax
from jax.experimental import pallas as pl
from jax.experimental.pallas import tpu as pltpu

F32 = jnp.float32
BF16 = jnp.bfloat16

RMS_EPS = 1e-6
LANES = 128
GDN_HEADS = 8
GDN_DK = 128
GDN_DV = 128
GDN_CONV = 4
GDN_CHUNK = 64
FOX_HEADS = 16
FOX_HD = 64
NEG_BIG = -1e30
LOG2_E = math.log2(math.e)
BIAS_TERMS = 6

ROW_TILE = 512
FFN_SPLITS = 1
GDN_PROJ_ROWS = 512
GDN_PROJ_SLAB = 256
GDN_CONV_ROWS = 256
GDN_TIME_BLOCK = 256
GDN_CHUNKS_PER_PREP = 4
ATTN_TILE = 512
ATTN_ROW_CHUNK = 512

VMEM_LIMIT = 56 * 1024 * 1024


def _cparams(*sem):
    return pltpu.CompilerParams(dimension_semantics=sem, vmem_limit_bytes=VMEM_LIMIT)


def _mm(a, b):
    return jnp.dot(a.astype(BF16), b.astype(BF16), preferred_element_type=F32)


def _mm_nt(a, b):
    return lax.dot_general(a.astype(BF16), b.astype(BF16), (((1,), (1,)), ((), ())),
                           preferred_element_type=F32)


def _rms_scale(x):
    return lax.rsqrt(jnp.mean(x * x, axis=-1, keepdims=True) + RMS_EPS)


def _sigmoid(x):
    return 1.0 / (1.0 + jnp.exp(-x))


def _softplus(x):
    return jnp.maximum(x, 0.0) + jnp.log(1.0 + jnp.exp(-jnp.abs(x)))


def _resident(shape, col_block=0):
    idx = (0,) * (len(shape) - 1) + (col_block,)
    return pl.BlockSpec(shape, lambda *_: idx, pipeline_mode=pl.Buffered(1))


def _gdn_in_proj_kernel(x_ref, g_ref, wm_ref, ws_ref, cw_ref, qkv_ref, z_ref, gate_ref, tail_ref,
                        *, tiles_per_seq):
    @pl.when(pl.program_id(0) % tiles_per_seq == 0)
    def _():
        tail_ref[0:8, :] = jnp.zeros((8, tail_ref.shape[1]), F32)

    x = x_ref[...]
    tm = x.shape[0]
    h = (x * _rms_scale(x) * g_ref[...]).astype(BF16)
    qk_w = GDN_HEADS * GDN_DK
    qkv_w = qkv_ref.shape[1]
    n_slabs = qkv_w // GDN_PROJ_SLAB
    z_slabs = z_ref.shape[1] // GDN_PROJ_SLAB
    z_every = n_slabs // z_slabs
    slab = lambda j: slice(j * GDN_PROJ_SLAB, (j + 1) * GDN_PROJ_SLAB)
    proj = lambda j: jnp.dot(h, wm_ref[:, slab(j)], preferred_element_type=F32)
    for j in range(n_slabs):
        cols = slab(j)
        y = proj(j)
        if j % z_every == z_every - 1:
            z_ref[:, slab(j // z_every)] = proj(n_slabs + j // z_every)
        tail_ref[8:8 + tm, cols] = y
        w = [cw_ref[kk:kk + 1, cols] for kk in range(GDN_CONV)]
        for r0 in range(0, tm, GDN_CONV_ROWS):
            xr = tail_ref[r0:r0 + 8 + GDN_CONV_ROWS, cols]
            x1 = pltpu.roll(xr, 1, axis=0)
            c = (xr * w[3] + x1 * w[2] + pltpu.roll(xr * w[1] + x1 * w[0], 2, axis=0))[8:]
            c = c * _sigmoid(c)
            if cols.start < 2 * qk_w:
                scale = GDN_DK ** -0.5 if cols.start < qk_w else 1.0
                heads = [c[:, i:i + GDN_DK] for i in range(0, GDN_PROJ_SLAB, GDN_DK)]
                c = jnp.concatenate(
                    [hd * (lax.rsqrt(jnp.sum(hd * hd, axis=-1, keepdims=True) + RMS_EPS) * scale)
                     for hd in heads], axis=1)
            qkv_ref[r0:r0 + GDN_CONV_ROWS, cols] = c
        tail_ref[0:8, cols] = tail_ref[tm:tm + 8, cols]
    gate_ref[...] = jnp.dot(h, ws_ref[...], preferred_element_type=F32)


def _gdn_in_proj(x2, gain, w_main, qkv_w, z_w, w_small, conv_w, tm, seq_len):
    n, d = x2.shape
    ns = w_small.shape[1]
    assert seq_len % tm == 0 and qkv_w % GDN_PROJ_SLAB == 0
    row = lambda i: (i, 0)
    return pl.pallas_call(
        functools.partial(_gdn_in_proj_kernel, tiles_per_seq=seq_len // tm),
        grid=(n // tm,),
        in_specs=[pl.BlockSpec((tm, d), row), _resident((1, d)), _resident((d, qkv_w + z_w)),
                  _resident((d, ns)), _resident((GDN_CONV, qkv_w))],
        out_specs=[pl.BlockSpec((tm, qkv_w), row), pl.BlockSpec((tm, z_w), row),
                   pl.BlockSpec((tm, ns), row)],
        out_shape=[jax.ShapeDtypeStruct((n, qkv_w), F32), jax.ShapeDtypeStruct((n, z_w), F32),
                   jax.ShapeDtypeStruct((n, ns), F32)],
        scratch_shapes=[pltpu.VMEM((8 + tm, qkv_w), F32)],
        compiler_params=_cparams("arbitrary"),
        name="gdn_in_proj",
    )(x2, gain, w_main, w_small, conv_w)


def _fox_proj_kernel(x_ref, gq_ref, gkv_ref, wq_ref, wk_ref, wv_ref, wf_ref, bf_ref,
                     q_ref, k_ref, v_ref, lf_ref):
    x = x_ref[...]
    xn = x * _rms_scale(x)
    hq = (xn * gq_ref[...]).astype(BF16)
    hkv = (xn * gkv_ref[...]).astype(BF16)
    scale = FOX_HD ** -0.5 * LOG2_E
    q_ref[...] = (jnp.dot(hq, wq_ref[...], preferred_element_type=F32) * scale).astype(BF16)
    k_ref[...] = jnp.dot(hkv, wk_ref[...], preferred_element_type=F32).astype(BF16)
    v_ref[...] = jnp.dot(hkv, wv_ref[...], preferred_element_type=F32).astype(BF16)
    f = jnp.dot(hkv, wf_ref[...], preferred_element_type=F32) + bf_ref[...]
    lf_ref[...] = -_softplus(-f)


def _fox_proj(x2, gq, gkv, wq, wkv, wf, bf, tm):
    n, d = x2.shape
    w = wq.shape[1]
    row = lambda i: (i, 0)
    return pl.pallas_call(
        _fox_proj_kernel,
        grid=(n // tm,),
        in_specs=[pl.BlockSpec((tm, d), row), _resident((1, d)), _resident((1, d)),
                  _resident((d, w)), _resident((d, w), 0), _resident((d, w), 1),
                  _resident((d, LANES)), _resident((1, LANES))],
        out_specs=[pl.BlockSpec((tm, w), row), pl.BlockSpec((tm, w), row),
                   pl.BlockSpec((tm, w), row), pl.BlockSpec((tm, LANES), row)],
        out_shape=[jax.ShapeDtypeStruct((n, w), BF16), jax.ShapeDtypeStruct((n, w), BF16),
                   jax.ShapeDtypeStruct((n, w), BF16), jax.ShapeDtypeStruct((n, LANES), F32)],
        compiler_params=_cparams("parallel"),
        name="fox_proj",
    )(x2, gq, gkv, wq, wkv, wkv, wf, bf)


def _mix_ffn_kernel(a_ref, wmix_ref, x_ref, gmix_ref, g1_ref, wg_ref, wu_ref, wo_ref, g2_ref,
                    o_ref, *, ff_splits):
    y = jnp.dot(a_ref[...], wmix_ref[...], preferred_element_type=F32)
    x = x_ref[...] + y * _rms_scale(y) * gmix_ref[...]
    h = (x * _rms_scale(x) * g1_ref[...]).astype(BF16)
    step = wg_ref.shape[1] // ff_splits
    f = None
    for i in range(ff_splits):
        cols = slice(i * step, (i + 1) * step)
        g = jnp.dot(h, wg_ref[:, cols], preferred_element_type=F32)
        u = jnp.dot(h, wu_ref[:, cols], preferred_element_type=F32)
        act = (g * _sigmoid(g) * u).astype(BF16)
        part = jnp.dot(act, wo_ref[cols, :], preferred_element_type=F32)
        f = part if f is None else f + part
    o_ref[...] = x + f * _rms_scale(f) * g2_ref[...]


def _mix_ffn(a, wmix, x2, gmix, g1, w_in, wo, g2, tm, ff_splits):
    n, d = x2.shape
    k = a.shape[1]
    dff = wo.shape[0]
    assert dff % (ff_splits * LANES) == 0
    row = lambda i: (i, 0)
    return pl.pallas_call(
        functools.partial(_mix_ffn_kernel, ff_splits=ff_splits),
        grid=(n // tm,),
        in_specs=[pl.BlockSpec((tm, k), row), _resident((k, d)), pl.BlockSpec((tm, d), row),
                  _resident((1, d)), _resident((1, d)), _resident((d, dff), 0),
                  _resident((d, dff), 1), _resident((dff, d)), _resident((1, d))],
        out_specs=pl.BlockSpec((tm, d), row),
        out_shape=jax.ShapeDtypeStruct((n, d), F32),
        compiler_params=_cparams("parallel"),
        name="mix_ffn",
    )(a, wmix, x2, gmix, g1, w_in, w_in, wo, g2)


def _gdn_kernel(qkv_ref, z_ref, gate_ref, alog_ref, dtb_ref, onorm_ref, o_ref,
                s_ref, beta_s, gcum_s, wq_s, u_s, aqk_s, kdt_s, gl_s, *, tb, chunks_per_prep):
    c = GDN_CHUNK
    n_chunks = tb // c
    nh = GDN_HEADS
    k_off = nh * GDN_DK
    v_off = 2 * nh * GDN_DK

    @pl.when(pl.program_id(1) == 0)
    def _():
        s_ref[...] = jnp.zeros_like(s_ref)

    gates = gate_ref[...]
    beta_s[...] = _sigmoid(gates)
    g = -jnp.exp(alog_ref[...]) * _softplus(gates + dtb_ref[...])
    row = lax.broadcasted_iota(jnp.int32, (tb, LANES), 0) & (c - 1)
    sh = 1
    while sh < c:
        g = g + jnp.where(row >= sh, pltpu.roll(g, sh, axis=0), 0.0)
        sh *= 2
    gcum_s[...] = g

    ri = lax.broadcasted_iota(jnp.int32, (c, c), 0)
    ci = lax.broadcasted_iota(jnp.int32, (c, c), 1)
    causal = ri >= ci
    strict = ri > ci
    eye = (ri == ci).astype(F32)
    level_masks = []
    s = 1
    while s < c:
        blk = ~(2 * s - 1)
        m = ((ri & blk) == (ci & blk)) & ((ri & (2 * s - 1)) >= s) & ((ci & (2 * s - 1)) < s)
        level_masks.append(m.astype(F32))
        s *= 2

    def prep_body(ip, carry):
        elems = []
        per_chunk = []
        for cc in range(chunks_per_prep):
            ic = ip * chunks_per_prep + cc
            rows = pl.ds(pl.multiple_of(ic * c, c), c)
            bt = beta_s[rows, :]
            gc = gcum_s[rows, :]
            g_last = gc[c - 1:c, :]
            per_chunk.append(dict(
                bt=bt, gc=gc, e=jnp.exp(gc), dk=jnp.exp(g_last - gc), gl=jnp.exp(g_last),
                gc_t=jnp.transpose(jnp.concatenate([gc, jnp.zeros_like(gc)], axis=0))))
            for h in range(nh):
                elems.append((cc, ic, rows, h))

        def col(name, cc, lane):
            return per_chunk[cc][name][:, lane:lane + 1]

        hcols = lambda h, off=0: slice(off + h * LANES, off + (h + 1) * LANES)
        kb = [qkv_ref[rows, hcols(h, k_off)].astype(BF16) for (_, _, rows, h) in elems]
        kk_m = [_mm_nt(x, x) for x in kb]
        decay = []
        for (cc, _, _, h) in elems:
            diff = col("gc", cc, nh + h) - per_chunk[cc]["gc_t"][nh + h:nh + h + 1, :c]
            decay.append(jnp.where(causal, jnp.exp(jnp.where(causal, diff, 0.0)), 0.0))
        a_strict = [jnp.where(strict, col("bt", cc, h) * kk_m[i] * decay[i], 0.0)
                    for i, (cc, _, _, h) in enumerate(elems)]
        d = [eye - a * level_masks[0] for a in a_strict]
        for m in level_masks[1:]:
            x = [_mm(a * m, di) for a, di in zip(a_strict, d)]
            d = [di - _mm(di, xi) for di, xi in zip(d, x)]
        for i, (cc, ic, rows, h) in enumerate(elems):
            kh = qkv_ref[rows, hcols(h, k_off)]
            b_col = col("bt", cc, h)
            rhs = jnp.concatenate([(b_col * col("e", cc, nh + h)) * kh,
                                   b_col * qkv_ref[rows, hcols(h, v_off)]], axis=1)
            sol = _mm(d[i], rhs)
            qh = qkv_ref[rows, hcols(h)]
            wq_s[ic, h, :c, :] = sol[:, :GDN_DK].astype(BF16)
            wq_s[ic, h, c:, :] = (qh * col("e", cc, nh + h)).astype(BF16)
            u_s[ic, h] = sol[:, GDN_DK:]
            aqk_s[ic, h] = (_mm_nt(qh, kb[i]) * decay[i]).astype(BF16)
            kdt_s[ic, h] = jnp.transpose(kh * col("dk", cc, nh + h)).astype(BF16)
            gl_s[ic, h] = jnp.broadcast_to(per_chunk[cc]["gl"][:, nh + h:nh + h + 1], (1, LANES))
        return carry

    lax.fori_loop(0, n_chunks // chunks_per_prep, prep_body, 0)

    def scan_body(ic, carry):
        rows = pl.ds(pl.multiple_of(ic * c, c), c)
        st = [s_ref[h] for h in range(nh)]
        wq = [_mm(wq_s[ic, h], st[h]) for h in range(nh)]
        v_new = [u_s[ic, h] - wq[h][:c] for h in range(nh)]
        v_new_b = [x.astype(BF16) for x in v_new]
        o = [wq[h][c:] + jnp.dot(aqk_s[ic, h], v_new_b[h], preferred_element_type=F32)
             for h in range(nh)]
        for h in range(nh):
            s_ref[h] = st[h] * gl_s[ic, h] + jnp.dot(kdt_s[ic, h], v_new_b[h],
                                                    preferred_element_type=F32)
        for h in range(nh):
            cols = slice(h * LANES, (h + 1) * LANES)
            zh = z_ref[rows, cols]
            on = o[h] * _rms_scale(o[h]) * onorm_ref[...]
            o_ref[rows, cols] = (on * (zh * _sigmoid(zh))).astype(o_ref.dtype)
        return carry

    lax.fori_loop(0, n_chunks, scan_body, 0)


def _gdn_core(qkv, z, gates, alog_pad, dtb_pad, out_norm, tb, chunks_per_prep):
    b, t, qkv_w = qkv.shape
    v_w = z.shape[2]
    c = GDN_CHUNK
    nc = tb // c
    blk = lambda w: pl.BlockSpec((None, tb, w), lambda i, j: (i, j, 0))
    return pl.pallas_call(
        functools.partial(_gdn_kernel, tb=tb, chunks_per_prep=chunks_per_prep),
        grid=(b, t // tb),
        in_specs=[blk(qkv_w), blk(v_w), blk(LANES),
                  _resident((1, LANES)), _resident((1, LANES)), _resident((1, GDN_DV))],
        out_specs=blk(v_w),
        out_shape=jax.ShapeDtypeStruct((b, t, v_w), BF16),
        scratch_shapes=[pltpu.VMEM((GDN_HEADS, GDN_DK, GDN_DV), F32),
                        pltpu.VMEM((tb, LANES), F32),
                        pltpu.VMEM((tb, LANES), F32),
                        pltpu.VMEM((nc, GDN_HEADS, 2 * c, GDN_DK), BF16),
                        pltpu.VMEM((nc, GDN_HEADS, c, GDN_DV), F32),
                        pltpu.VMEM((nc, GDN_HEADS, c, c), BF16),
                        pltpu.VMEM((nc, GDN_HEADS, GDN_DK, c), BF16),
                        pltpu.VMEM((nc, GDN_HEADS, 1, LANES), F32)],
        compiler_params=_cparams("parallel", "arbitrary"),
        name="gdn_core",
    )(qkv, z, gates, alog_pad, dtb_pad, out_norm)


def _bias_terms_kernel(x_ref, selq_ref, selk_ref, oneq_ref, onek_ref, qa_ref, ka_ref):
    x = x_ref[...]
    t = x.shape[0]
    row = lax.broadcasted_iota(jnp.int32, x.shape, 0)
    sh = 1
    while sh < t:
        x = x + jnp.where(row >= sh, pltpu.roll(x, sh, axis=0), 0.0)
        sh *= 2
    c = x * LOG2_E
    hi = c.astype(BF16)
    r = c - hi.astype(F32)
    mid = r.astype(BF16)
    lo = (r - mid.astype(F32)).astype(BF16)
    parts = (hi, mid, lo)
    qa = oneq_ref[...]
    ka = onek_ref[...]
    for i in range(3):
        qa = qa + jnp.dot(parts[i], selq_ref[i], preferred_element_type=F32)
        ka = ka - jnp.dot(parts[i], selk_ref[i], preferred_element_type=F32)
    qa_ref[...] = qa.astype(BF16)
    ka_ref[...] = ka.astype(BF16)


def _bias_layout():
    w = LANES
    assert FOX_HEADS * BIAS_TERMS <= w
    selq = np.zeros((3, LANES, w), np.float32)
    selk = np.zeros((3, LANES, w), np.float32)
    oneq = np.zeros((1, w), np.float32)
    onek = np.zeros((1, w), np.float32)
    for h in range(FOX_HEADS):
        base = BIAS_TERMS * h
        for i in range(3):
            selq[i, h, base + i] = 1.0
            selk[i, h, base + 3 + i] = 1.0
            oneq[0, base + 3 + i] = 1.0
            onek[0, base + i] = 1.0
    return (jnp.asarray(selq, BF16), jnp.asarray(selk, BF16), jnp.asarray(oneq), jnp.asarray(onek))


def _bias_terms(lf):
    b, t, w = lf.shape
    selq, selk, oneq, onek = _bias_layout()
    seq = pl.BlockSpec((None, t, w), lambda i: (i, 0, 0))
    return pl.pallas_call(
        _bias_terms_kernel,
        grid=(b,),
        in_specs=[seq, _resident((3, LANES, w)), _resident((3, LANES, w)),
                  _resident((1, w)), _resident((1, w))],
        out_specs=[seq, seq],
        out_shape=[jax.ShapeDtypeStruct((b, t, w), BF16), jax.ShapeDtypeStruct((b, t, w), BF16)],
        compiler_params=_cparams("parallel"),
        name="forget_bias_terms",
    )(lf, selq, selk, oneq, onek)


def _attn_kernel(q_ref, qa_ref, k_ref, ka_ref, v_ref, o_ref, q2_buf, s_buf, r_buf, m_buf, acc_buf,
                 *, tq):
    n_q = q_ref.shape[0] // tq
    lane = lax.broadcasted_iota(jnp.int32, (1, LANES), 1)
    ones = jnp.ones((tq, LANES), BF16)
    rc = ATTN_ROW_CHUNK
    chunks = [slice(r0, r0 + rc) for r0 in range(0, 2 * tq, rc)]
    n_tiles = tq // LANES
    ri = lax.broadcasted_iota(jnp.int32, (rc, tq), 0)
    ci = lax.broadcasted_iota(jnp.int32, (rc, tq), 1)

    def key_rows(j):
        return pl.ds(pl.multiple_of(j * tq, tq), tq)

    def tile_rows(qi, ch):
        return pl.ds(pl.multiple_of(qi * tq + (ch.start & (tq - 1)), rc), rc)

    def stacked_q(qi, ch):
        p = ch.start // tq
        q = q_ref[tile_rows(qi, ch), :]
        qa = qa_ref[tile_rows(qi, ch), :]
        head = (lane >> 6) == p
        b0 = BIAS_TERMS * (2 * pl.program_id(1) + p)
        bias = (lane >= b0) & (lane < b0 + BIAS_TERMS)
        zero = jnp.zeros_like(q)
        return jnp.concatenate([jnp.where(head, q, zero), jnp.where(bias, qa, zero)], axis=1)

    def logits(rows, ch):
        kf = jnp.concatenate([k_ref[rows, :], ka_ref[rows, :]], axis=1)
        return lax.dot_general(q2_buf[ch, :], kf, (((1,), (1,)), ((), ())),
                               preferred_element_type=F32)

    def row_max(s):
        return jnp.broadcast_to(jnp.max(s, axis=-1, keepdims=True), (s.shape[0], LANES))

    def absorb(ch, rows):
        vf = jnp.concatenate([v_ref[rows, :], ones], axis=1)
        s, m = s_buf[ch, :], m_buf[ch, :]
        m_new = jnp.maximum(m, r_buf[ch, :])
        alpha = jnp.exp2(m - m_new)
        pm = jnp.concatenate([jnp.exp2(s[:, i * LANES:(i + 1) * LANES] - m_new).astype(BF16)
                              for i in range(n_tiles)], axis=1)
        pv = jnp.dot(pm, vf, preferred_element_type=F32)
        return m_new, jnp.concatenate([alpha, alpha], axis=1) * acc_buf[ch, :] + pv

    def refill(j, ch):
        sn = logits(key_rows(j), ch)
        s_buf[ch, :] = sn
        r_buf[ch, :] = row_max(sn)

    def reset(ch):
        m_buf[ch, :] = jnp.full((rc, LANES), NEG_BIG, F32)
        acc_buf[ch, :] = jnp.zeros((rc, 2 * LANES), F32)

    for ch in chunks:
        q2_buf[ch, :] = stacked_q(0, ch)
        reset(ch)
        refill(0, ch)

    def tile(qi, carry):
        def step(j):
            for ch in chunks:
                m_new, acc_new = absorb(ch, key_rows(j))
                m_buf[ch, :] = m_new
                acc_buf[ch, :] = acc_new
                refill(j + 1, ch)

        def quad_step(i, c):
            for u in range(4):
                step(4 * i + u)
            return c

        lax.fori_loop(0, qi >> 2, quad_step, 0)
        done = qi & ~3

        @pl.when((qi & 2) != 0)
        def _():
            step(done)
            step(done + 1)

        @pl.when((qi & 1) != 0)
        def _():
            step(qi - 1)

        for ch in chunks:
            s = jnp.where(ri + (ch.start & (tq - 1)) >= ci, s_buf[ch, :], NEG_BIG)
            s_buf[ch, :] = s
            r_buf[ch, :] = row_max(s)
        q_next = jnp.minimum(qi + 1, n_q - 1)
        for ch in chunks:
            _, acc = absorb(ch, key_rows(qi))
            o = (acc[:, :LANES] / acc[:, LANES:]).astype(o_ref.dtype)
            if ch.start < tq:
                o_ref[tile_rows(qi, ch), :] = o
            else:
                o_ref[tile_rows(qi, ch), :] = jnp.where((lane >> 6) == 1, o,
                                                        o_ref[tile_rows(qi, ch), :])
            reset(ch)
            q2_buf[ch, :] = stacked_q(q_next, ch)
            refill(0, ch)
        return carry

    lax.fori_loop(0, n_q, tile, 0)


def _fox_attention(q, qa, k, ka, v, tq):
    b, t, w = q.shape
    n_pairs = w // LANES
    slab = pl.BlockSpec((None, t, LANES), lambda i, p: (i, 0, p))
    bias_slab = pl.BlockSpec((None, t, LANES), lambda i, p: (i, 0, 0))
    return pl.pallas_call(
        functools.partial(_attn_kernel, tq=tq),
        grid=(b, n_pairs),
        in_specs=[slab, bias_slab, slab, bias_slab, slab],
        out_specs=slab,
        out_shape=jax.ShapeDtypeStruct((b, t, w), BF16),
        scratch_shapes=[pltpu.VMEM((2 * tq, 2 * LANES), BF16),
                        pltpu.VMEM((2 * tq, tq), F32),
                        pltpu.VMEM((2 * tq, LANES), F32),
                        pltpu.VMEM((2 * tq, LANES), F32),
                        pltpu.VMEM((2 * tq, 2 * LANES), F32)],
        compiler_params=_cparams("parallel", "parallel"),
        name="fox_attention",
    )(q, qa, k, ka, v)


def _pad_lanes(a):
    return jnp.pad(a, [(0, 0)] * (a.ndim - 1) + [(0, LANES - a.shape[-1])])


def kernel(x, pre_mix_norm, post_mix_norm, pre_ffn_norm, post_ffn_norm, w_ffn_in, w_ffn_out,
           gdn_w_in, gdn_conv, gdn_a_log, gdn_dt_bias, gdn_out_norm, gdn_w_out,
           kv_norm, w_kv, b_forget, fox_w_q, fox_w_o):
    b, t, d = x.shape
    n = b * t
    assert gdn_w_in.shape[0] == 1 and fox_w_q.shape[0] == 1, "one layer of each mixer"
    x2 = x.reshape(n, d)
    row = lambda a: a.reshape(1, -1).astype(F32)

    def mix_ffn(mix, w_mix, x2, layer):
        return _mix_ffn(mix, w_mix.astype(BF16), x2, row(post_mix_norm[layer]),
                        row(pre_ffn_norm[layer]), w_ffn_in[layer].astype(BF16),
                        w_ffn_out[layer].astype(BF16), row(post_ffn_norm[layer]),
                        ROW_TILE, FFN_SPLITS)

    qkv_w = 3 * GDN_HEADS * GDN_DK
    z_w = GDN_HEADS * GDN_DV
    w_in = gdn_w_in[0]
    w_small = _pad_lanes(w_in[:, qkv_w + z_w:]).astype(BF16)
    qkv, z, gates = _gdn_in_proj(x2, row(pre_mix_norm[0]), w_in.astype(BF16), qkv_w, z_w, w_small,
                                 gdn_conv[0].astype(F32), GDN_PROJ_ROWS, t)
    zeros_h = jnp.zeros((GDN_HEADS,), F32)
    alog_pad = _pad_lanes(jnp.concatenate([zeros_h, gdn_a_log[0].astype(F32)])[None, :])
    dtb_pad = _pad_lanes(jnp.concatenate([zeros_h, gdn_dt_bias[0].astype(F32)])[None, :])
    og = _gdn_core(qkv.reshape(b, t, qkv_w), z.reshape(b, t, z_w), gates.reshape(b, t, LANES),
                   alog_pad, dtb_pad, row(gdn_out_norm[0]), GDN_TIME_BLOCK, GDN_CHUNKS_PER_PREP)
    x2 = mix_ffn(og.reshape(n, -1), gdn_w_out[0], x2, 0)

    fw = FOX_HEADS * FOX_HD
    q, k, v, lf = _fox_proj(x2, row(pre_mix_norm[1]), row(kv_norm), fox_w_q[0].astype(BF16),
                            w_kv.astype(BF16), _pad_lanes(w_kv[:, 2 * fw:]).astype(BF16),
                            _pad_lanes(b_forget.astype(F32)[None, :]), ROW_TILE)
    qa, ka = _bias_terms(lf.reshape(b, t, LANES))
    to3 = lambda a: a.reshape(b, t, -1)
    oa = _fox_attention(to3(q), to3(qa), to3(k), to3(ka), to3(v), ATTN_TILE)
    x2 = mix_ffn(oa.reshape(n, fw), fox_w_o[0], x2, 1)
    return x2.reshape(b, t, d)
```

```python
import functools
import math

import jax
import jax.numpy as jnp
import numpy as np
from jax import lax
from jax.experimental import pallas as pl
from jax.experimental.pallas import tpu as pltpu

F32 = jnp.float32
BF16 = jnp.bfloat16

RMS_EPS = 1e-6
LANES = 128
GDN_HEADS = 8
GDN_DK = 128
GDN_DV = 128
GDN_CONV = 4
GDN_CHUNK = 64
FOX_HEADS = 16
FOX_HD = 64
NEG_BIG = -1e30
LOG2_E = math.log2(math.e)
BIAS_TERMS = 6

ROW_TILE = 512
FFN_SPLITS = 1
GDN_PROJ_ROWS = 512
GDN_PROJ_SLAB = 256
GDN_CONV_ROWS = 256
GDN_TIME_BLOCK = 256
GDN_CHUNKS_PER_PREP = 4
ATTN_TILE = 512
ATTN_ROW_CHUNK = 512

VMEM_LIMIT = 56 * 1024 * 1024


def _cparams(*sem):
    return pltpu.CompilerParams(dimension_semantics=sem, vmem_limit_bytes=VMEM_LIMIT)


def _mm(a, b):
    return jnp.dot(a.astype(BF16), b.astype(BF16), preferred_element_type=F32)


def _mm_nt(a, b):
    return lax.dot_general(a.astype(BF16), b.astype(BF16), (((1,), (1,)), ((), ())),
                           preferred_element_type=F32)


def _rms_scale(x):
    return lax.rsqrt(jnp.mean(x * x, axis=-1, keepdims=True) + RMS_EPS)


def _sigmoid(x):
    return 1.0 / (1.0 + jnp.exp2(x * (-LOG2_E)))


def _softplus(x):
    return jnp.maximum(x, 0.0) + jnp.log(1.0 + jnp.exp(-jnp.abs(x)))


def _resident(shape, col_block=0, layer=None):
    idx = (0,) * (len(shape) - 1) + (col_block,)
    if layer is not None:
        shape, idx = (None,) + tuple(shape), (layer,) + idx
    return pl.BlockSpec(shape, lambda *_: idx, pipeline_mode=pl.Buffered(1))


def _gdn_in_proj_kernel(x_ref, g_ref, wm_ref, ws_ref, cw_ref, qkv_ref, z_ref, gate_ref, tail_ref,
                        *, tiles_per_seq):
    @pl.when(pl.program_id(0) % tiles_per_seq == 0)
    def _():
        tail_ref[0:8, :] = jnp.zeros((8, tail_ref.shape[1]), F32)

    x = x_ref[...]
    tm = x.shape[0]
    h = (x * _rms_scale(x) * g_ref[...]).astype(BF16)
    qk_w = GDN_HEADS * GDN_DK
    qkv_w = qkv_ref.shape[1]
    n_slabs = qkv_w // GDN_PROJ_SLAB
    z_slabs = z_ref.shape[1] // GDN_PROJ_SLAB
    z_every = n_slabs // z_slabs
    slab = lambda j: slice(j * GDN_PROJ_SLAB, (j + 1) * GDN_PROJ_SLAB)
    proj = lambda j: jnp.dot(h, wm_ref[:, slab(j)], preferred_element_type=F32)
    for j in range(n_slabs):
        cols = slab(j)
        y = proj(j)
        if j % z_every == z_every - 1:
            z_ref[:, slab(j // z_every)] = proj(n_slabs + j // z_every)
        tail_ref[8:8 + tm, cols] = y
        w = [cw_ref[kk:kk + 1, cols] for kk in range(GDN_CONV)]
        for r0 in range(0, tm, GDN_CONV_ROWS):
            xr = tail_ref[r0:r0 + 8 + GDN_CONV_ROWS, cols]
            x1 = pltpu.roll(xr, 1, axis=0)
            c = (xr * w[3] + x1 * w[2] + pltpu.roll(xr * w[1] + x1 * w[0], 2, axis=0))[8:]
            c = c * _sigmoid(c)
            if cols.start < 2 * qk_w:
                scale = GDN_DK ** -0.5 if cols.start < qk_w else 1.0
                heads = [c[:, i:i + GDN_DK] for i in range(0, GDN_PROJ_SLAB, GDN_DK)]
                c = jnp.concatenate(
                    [hd * (lax.rsqrt(jnp.sum(hd * hd, axis=-1, keepdims=True) + RMS_EPS) * scale)
                     for hd in heads], axis=1)
            qkv_ref[r0:r0 + GDN_CONV_ROWS, cols] = c
        tail_ref[0:8, cols] = tail_ref[tm:tm + 8, cols]
    gate_ref[...] = jnp.dot(h, ws_ref[...], preferred_element_type=F32)


def _gdn_in_proj(x2, gain, w_main, qkv_w, z_w, w_small, conv_w, tm, seq_len):
    n, d = x2.shape
    ns = w_small.shape[1]
    assert seq_len % tm == 0 and qkv_w % GDN_PROJ_SLAB == 0
    row = lambda i: (i, 0)
    return pl.pallas_call(
        functools.partial(_gdn_in_proj_kernel, tiles_per_seq=seq_len // tm),
        grid=(n // tm,),
        in_specs=[pl.BlockSpec((tm, d), row), _resident((1, d)), _resident((d, qkv_w + z_w)),
                  _resident((d, ns)), _resident((GDN_CONV, qkv_w))],
        out_specs=[pl.BlockSpec((tm, qkv_w), row), pl.BlockSpec((tm, z_w), row),
                   pl.BlockSpec((tm, ns), row)],
        out_shape=[jax.ShapeDtypeStruct((n, qkv_w), F32), jax.ShapeDtypeStruct((n, z_w), F32),
                   jax.ShapeDtypeStruct((n, ns), F32)],
        scratch_shapes=[pltpu.VMEM((8 + tm, qkv_w), F32)],
        compiler_params=_cparams("arbitrary"),
        name="gdn_in_proj",
    )(x2, gain, w_main, w_small, conv_w)


def _fox_proj_kernel(x_ref, gq_ref, gkv_ref, wq_ref, wk_ref, wv_ref, wf_ref, bf_ref,
                     q_ref, k_ref, v_ref, lf_ref):
    x = x_ref[...]
    xn = x * _rms_scale(x)
    hq = (xn * gq_ref[...]).astype(BF16)
    hkv = (xn * gkv_ref[...]).astype(BF16)
    scale = FOX_HD ** -0.5 * LOG2_E
    q_ref[...] = (jnp.dot(hq, wq_ref[...], preferred_element_type=F32) * scale).astype(BF16)
    k_ref[...] = jnp.dot(hkv, wk_ref[...], preferred_element_type=F32).astype(BF16)
    v_ref[...] = jnp.dot(hkv, wv_ref[...], preferred_element_type=F32).astype(BF16)
    f = jnp.dot(hkv, wf_ref[...], preferred_element_type=F32) + bf_ref[...]
    lf_ref[...] = -_softplus(-f)


def _fox_proj(x2, gq, gkv, wq, wkv, wf, bf, tm):
    n, d = x2.shape
    w = wq.shape[1]
    row = lambda i: (i, 0)
    return pl.pallas_call(
        _fox_proj_kernel,
        grid=(n // tm,),
        in_specs=[pl.BlockSpec((tm, d), row), _resident((1, d)), _resident((1, d)),
                  _resident((d, w)), _resident((d, w), 0), _resident((d, w), 1),
                  _resident((d, LANES)), _resident((1, LANES))],
        out_specs=[pl.BlockSpec((tm, w), row), pl.BlockSpec((tm, w), row),
                   pl.BlockSpec((tm, w), row), pl.BlockSpec((tm, LANES), row)],
        out_shape=[jax.ShapeDtypeStruct((n, w), BF16), jax.ShapeDtypeStruct((n, w), BF16),
                   jax.ShapeDtypeStruct((n, w), BF16), jax.ShapeDtypeStruct((n, LANES), F32)],
        compiler_params=_cparams("parallel"),
        name="fox_proj",
    )(x2, gq, gkv, wq, wkv, wkv, wf, bf)


def _mix_ffn_kernel(a_ref, wmix_ref, x_ref, gmix_ref, g1_ref, wg_ref, wu_ref, wo_ref, g2_ref,
                    o_ref, *, ff_splits):
    y = jnp.dot(a_ref[...], wmix_ref[...], preferred_element_type=F32)
    x = x_ref[...] + y * _rms_scale(y) * gmix_ref[...]
    h = (x * _rms_scale(x) * g1_ref[...]).astype(BF16)
    step = wg_ref.shape[1] // ff_splits
    f = None
    for i in range(ff_splits):
        cols = slice(i * step, (i + 1) * step)
        g = jnp.dot(h, wg_ref[:, cols], preferred_element_type=F32)
        u = jnp.dot(h, wu_ref[:, cols], preferred_element_type=F32)
        act = (g * _sigmoid(g) * u).astype(BF16)
        part = jnp.dot(act, wo_ref[cols, :], preferred_element_type=F32)
        f = part if f is None else f + part
    o_ref[...] = x + f * _rms_scale(f) * g2_ref[...]


def _mix_ffn(a, wmix, x2, gmix, g1, w_in, wo, g2, layer, tm, ff_splits):
    n, d = x2.shape
    k = a.shape[1]
    dff = wo.shape[1]
    assert dff % (ff_splits * LANES) == 0
    row = lambda i: (i, 0)
    return pl.pallas_call(
        functools.partial(_mix_ffn_kernel, ff_splits=ff_splits),
        grid=(n // tm,),
        in_specs=[pl.BlockSpec((tm, k), row), _resident((k, d)), pl.BlockSpec((tm, d), row),
                  _resident((1, d)), _resident((1, d)), _resident((d, dff), 0, layer),
                  _resident((d, dff), 1, layer), _resident((dff, d), 0, layer), _resident((1, d))],
        out_specs=pl.BlockSpec((tm, d), row),
        out_shape=jax.ShapeDtypeStruct((n, d), F32),
        compiler_params=_cparams("parallel"),
        name="mix_ffn",
    )(a, wmix, x2, gmix, g1, w_in, w_in, wo, g2)


def _gdn_kernel(qkv_ref, z_ref, gate_ref, alog_ref, dtb_ref, onorm_ref, o_ref,
                s_ref, beta_s, gcum_s, wq_s, u_s, aqk_s, kdt_s, gl_s, *, tb, chunks_per_prep):
    c = GDN_CHUNK
    n_chunks = tb // c
    nh = GDN_HEADS
    k_off = nh * GDN_DK
    v_off = 2 * nh * GDN_DK

    @pl.when(pl.program_id(1) == 0)
    def _():
        s_ref[...] = jnp.zeros_like(s_ref)

    gates = gate_ref[...]
    beta_s[...] = _sigmoid(gates)
    g = -jnp.exp(alog_ref[...]) * _softplus(gates + dtb_ref[...])
    row = lax.broadcasted_iota(jnp.int32, (tb, LANES), 0) & (c - 1)
    sh = 1
    while sh < c:
        g = g + jnp.where(row >= sh, pltpu.roll(g, sh, axis=0), 0.0)
        sh *= 2
    gcum_s[...] = g

    ri = lax.broadcasted_iota(jnp.int32, (c, c), 0)
    ci = lax.broadcasted_iota(jnp.int32, (c, c), 1)
    causal = ri >= ci
    strict = ri > ci
    eye = (ri == ci).astype(F32)
    level_masks = []
    s = 1
    while s < c:
        blk = ~(2 * s - 1)
        m = ((ri & blk) == (ci & blk)) & ((ri & (2 * s - 1)) >= s) & ((ci & (2 * s - 1)) < s)
        level_masks.append(m.astype(F32))
        s *= 2

    def prep_body(ip, carry):
        elems = []
        per_chunk = []
        for cc in range(chunks_per_prep):
            ic = ip * chunks_per_prep + cc
            rows = pl.ds(pl.multiple_of(ic * c, c), c)
            bt = beta_s[rows, :]
            gc = gcum_s[rows, :]
            g_last = gc[c - 1:c, :]
            per_chunk.append(dict(
                bt=bt, gc=gc, e=jnp.exp(gc), dk=jnp.exp(g_last - gc), gl=jnp.exp(g_last),
                gc_t=jnp.transpose(jnp.concatenate([gc, jnp.zeros_like(gc)], axis=0))))
            for h in range(nh):
                elems.append((cc, ic, rows, h))

        def col(name, cc, lane):
            return per_chunk[cc][name][:, lane:lane + 1]

        hcols = lambda h, off=0: slice(off + h * LANES, off + (h + 1) * LANES)
        kb = [qkv_ref[rows, hcols(h, k_off)].astype(BF16) for (_, _, rows, h) in elems]
        kk_m = [_mm_nt(x, x) for x in kb]
        decay = []
        for (cc, _, _, h) in elems:
            diff = col("gc", cc, nh + h) - per_chunk[cc]["gc_t"][nh + h:nh + h + 1, :c]
            decay.append(jnp.where(causal, jnp.exp(jnp.where(causal, diff, 0.0)), 0.0))
        a_strict = [jnp.where(strict, col("bt", cc, h) * kk_m[i] * decay[i], 0.0)
                    for i, (cc, _, _, h) in enumerate(elems)]
        d = [eye - a * level_masks[0] for a in a_strict]
        for m in level_masks[1:]:
            x = [_mm(a * m, di) for a, di in zip(a_strict, d)]
            d = [di - _mm(di, xi) for di, xi in zip(d, x)]
        for i, (cc, ic, rows, h) in enumerate(elems):
            kh = qkv_ref[rows, hcols(h, k_off)]
            b_col = col("bt", cc, h)
            rhs = jnp.concatenate([(b_col * col("e", cc, nh + h)) * kh,
                                   b_col * qkv_ref[rows, hcols(h, v_off)]], axis=1)
            sol = _mm(d[i], rhs)
            qh = qkv_ref[rows, hcols(h)]
            wq_s[ic, h, :c, :] = sol[:, :GDN_DK].astype(BF16)
            wq_s[ic, h, c:, :] = (qh * col("e", cc, nh + h)).astype(BF16)
            u_s[ic, h] = sol[:, GDN_DK:]
            aqk_s[ic, h] = (_mm_nt(qh, kb[i]) * decay[i]).astype(BF16)
            kdt_s[ic, h] = jnp.transpose(kh * col("dk", cc, nh + h)).astype(BF16)
            gl_s[ic, h] = jnp.broadcast_to(per_chunk[cc]["gl"][:, nh + h:nh + h + 1], (1, LANES))
        return carry

    lax.fori_loop(0, n_chunks // chunks_per_prep, prep_body, 0)

    def scan_body(ic, carry):
        rows = pl.ds(pl.multiple_of(ic * c, c), c)
        st = [s_ref[h] for h in range(nh)]
        wq = [_mm(wq_s[ic, h], st[h]) for h in range(nh)]
        v_new = [u_s[ic, h] - wq[h][:c] for h in range(nh)]
        v_new_b = [x.astype(BF16) for x in v_new]
        o = [wq[h][c:] + jnp.dot(aqk_s[ic, h], v_new_b[h], preferred_element_type=F32)
             for h in range(nh)]
        for h in range(nh):
            s_ref[h] = st[h] * gl_s[ic, h] + jnp.dot(kdt_s[ic, h], v_new_b[h],
                                                    preferred_element_type=F32)
        for h in range(nh):
            cols = slice(h * LANES, (h + 1) * LANES)
            zh = z_ref[rows, cols]
            on = o[h] * _rms_scale(o[h]) * onorm_ref[...]
            o_ref[rows, cols] = (on * (zh * _sigmoid(zh))).astype(o_ref.dtype)
        return carry

    lax.fori_loop(0, n_chunks, scan_body, 0)


def _gdn_core(qkv, z, gates, alog_pad, dtb_pad, out_norm, tb, chunks_per_prep):
    b, t, qkv_w = qkv.shape
    v_w = z.shape[2]
    c = GDN_CHUNK
    nc = tb // c
    blk = lambda w: pl.BlockSpec((None, tb, w), lambda i, j: (i, j, 0))
    return pl.pallas_call(
        functools.partial(_gdn_kernel, tb=tb, chunks_per_prep=chunks_per_prep),
        grid=(b, t // tb),
        in_specs=[blk(qkv_w), blk(v_w), blk(LANES),
                  _resident((1, LANES)), _resident((1, LANES)), _resident((1, GDN_DV))],
        out_specs=blk(v_w),
        out_shape=jax.ShapeDtypeStruct((b, t, v_w), BF16),
        scratch_shapes=[pltpu.VMEM((GDN_HEADS, GDN_DK, GDN_DV), F32),
                        pltpu.VMEM((tb, LANES), F32),
                        pltpu.VMEM((tb, LANES), F32),
                        pltpu.VMEM((nc, GDN_HEADS, 2 * c, GDN_DK), BF16),
                        pltpu.VMEM((nc, GDN_HEADS, c, GDN_DV), F32),
                        pltpu.VMEM((nc, GDN_HEADS, c, c), BF16),
                        pltpu.VMEM((nc, GDN_HEADS, GDN_DK, c), BF16),
                        pltpu.VMEM((nc, GDN_HEADS, 1, LANES), F32)],
        compiler_params=_cparams("parallel", "arbitrary"),
        name="gdn_core",
    )(qkv, z, gates, alog_pad, dtb_pad, out_norm)


def _bias_terms_kernel(x_ref, selq_ref, selk_ref, oneq_ref, onek_ref, qa_ref, ka_ref):
    x = x_ref[...]
    t = x.shape[0]
    row = lax.broadcasted_iota(jnp.int32, x.shape, 0)
    sh = 1
    while sh < t:
        x = x + jnp.where(row >= sh, pltpu.roll(x, sh, axis=0), 0.0)
        sh *= 2
    c = x * LOG2_E
    hi = c.astype(BF16)
    r = c - hi.astype(F32)
    mid = r.astype(BF16)
    lo = (r - mid.astype(F32)).astype(BF16)
    parts = (hi, mid, lo)
    qa = oneq_ref[...]
    ka = onek_ref[...]
    for i in range(3):
        qa = qa + jnp.dot(parts[i], selq_ref[i], preferred_element_type=F32)
        ka = ka - jnp.dot(parts[i], selk_ref[i], preferred_element_type=F32)
    qa_ref[...] = qa.astype(BF16)
    ka_ref[...] = ka.astype(BF16)


def _bias_layout():
    w = LANES
    assert FOX_HEADS * BIAS_TERMS <= w
    selq = np.zeros((3, LANES, w), np.float32)
    selk = np.zeros((3, LANES, w), np.float32)
    oneq = np.zeros((1, w), np.float32)
    onek = np.zeros((1, w), np.float32)
    for h in range(FOX_HEADS):
        base = BIAS_TERMS * h
        for i in range(3):
            selq[i, h, base + i] = 1.0
            selk[i, h, base + 3 + i] = 1.0
            oneq[0, base + 3 + i] = 1.0
            onek[0, base + i] = 1.0
    return (jnp.asarray(selq, BF16), jnp.asarray(selk, BF16), jnp.asarray(oneq), jnp.asarray(onek))


def _bias_terms(lf):
    b, t, w = lf.shape
    selq, selk, oneq, onek = _bias_layout()
    seq = pl.BlockSpec((None, t, w), lambda i: (i, 0, 0))
    return pl.pallas_call(
        _bias_terms_kernel,
        grid=(b,),
        in_specs=[seq, _resident((3, LANES, w)), _resident((3, LANES, w)),
                  _resident((1, w)), _resident((1, w))],
        out_specs=[seq, seq],
        out_shape=[jax.ShapeDtypeStruct((b, t, w), BF16), jax.ShapeDtypeStruct((b, t, w), BF16)],
        compiler_params=_cparams("parallel"),
        name="forget_bias_terms",
    )(lf, selq, selk, oneq, onek)


def _attn_kernel(q_ref, qa_ref, k_ref, ka_ref, v_ref, o_ref, q2_buf, s_buf, r_buf, m_buf, acc_buf,
                 *, tq):
    n_q = q_ref.shape[0] // tq
    lane = lax.broadcasted_iota(jnp.int32, (1, LANES), 1)
    ones = jnp.ones((tq, LANES), BF16)
    rc = ATTN_ROW_CHUNK
    assert tq % rc == 0, "a row chunk must stay inside one head of the stacked tile"
    chunks = [slice(r0, r0 + rc) for r0 in range(0, 2 * tq, rc)]
    n_tiles = tq // LANES
    ri = lax.broadcasted_iota(jnp.int32, (rc, tq), 0)
    ci = lax.broadcasted_iota(jnp.int32, (rc, tq), 1)

    def key_rows(j):
        return pl.ds(pl.multiple_of(j * tq, tq), tq)

    def tile_rows(qi, ch):
        return pl.ds(pl.multiple_of(qi * tq + (ch.start & (tq - 1)), rc), rc)

    def stacked_q(qi, ch):
        p = ch.start // tq
        q = q_ref[tile_rows(qi, ch), :]
        qa = qa_ref[tile_rows(qi, ch), :]
        head = (lane >> 6) == p
        b0 = BIAS_TERMS * (2 * pl.program_id(1) + p)
        bias = (lane >= b0) & (lane < b0 + BIAS_TERMS)
        zero = jnp.zeros_like(q)
        return jnp.concatenate([jnp.where(head, q, zero), jnp.where(bias, qa, zero)], axis=1)

    def logits(rows, ch):
        kf = jnp.concatenate([k_ref[rows, :], ka_ref[rows, :]], axis=1)
        return lax.dot_general(q2_buf[ch, :], kf, (((1,), (1,)), ((), ())),
                               preferred_element_type=F32)

    def row_max(s):
        return jnp.broadcast_to(jnp.max(s, axis=-1, keepdims=True), (s.shape[0], LANES))

    def absorb(ch, rows):
        vf = jnp.concatenate([v_ref[rows, :], ones], axis=1)
        s, m = s_buf[ch, :], m_buf[ch, :]
        m_new = jnp.maximum(m, r_buf[ch, :])
        alpha = jnp.exp2(m - m_new)
        pm = jnp.concatenate([jnp.exp2(s[:, i * LANES:(i + 1) * LANES] - m_new).astype(BF16)
                              for i in range(n_tiles)], axis=1)
        pv = jnp.dot(pm, vf, preferred_element_type=F32)
        return m_new, jnp.concatenate([alpha, alpha], axis=1) * acc_buf[ch, :] + pv

    def refill(j, ch):
        sn = logits(key_rows(j), ch)
        s_buf[ch, :] = sn
        r_buf[ch, :] = row_max(sn)

    def reset(ch):
        m_buf[ch, :] = jnp.full((rc, LANES), NEG_BIG, F32)
        acc_buf[ch, :] = jnp.zeros((rc, 2 * LANES), F32)

    for ch in chunks:
        q2_buf[ch, :] = stacked_q(0, ch)
        reset(ch)
        refill(0, ch)

    def tile(qi, carry):
        def step(j):
            for ch in chunks:
                m_new, acc_new = absorb(ch, key_rows(j))
                m_buf[ch, :] = m_new
                acc_buf[ch, :] = acc_new
                refill(j + 1, ch)

        def quad_step(i, c):
            for u in range(4):
                step(4 * i + u)
            return c

        lax.fori_loop(0, qi >> 2, quad_step, 0)
        done = qi & ~3

        @pl.when((qi & 2) != 0)
        def _():
            step(done)
            step(done + 1)

        @pl.when((qi & 1) != 0)
        def _():
            step(qi - 1)

        for ch in chunks:
            s = jnp.where(ri + (ch.start & (tq - 1)) >= ci, s_buf[ch, :], NEG_BIG)
            s_buf[ch, :] = s
            r_buf[ch, :] = row_max(s)
        q_next = jnp.minimum(qi + 1, n_q - 1)
        for ch in chunks:
            _, acc = absorb(ch, key_rows(qi))
            o = (acc[:, :LANES] / acc[:, LANES:]).astype(o_ref.dtype)
            if ch.start < tq:
                o_ref[tile_rows(qi, ch), :] = o
            else:
                o_ref[tile_rows(qi, ch), :] = jnp.where((lane >> 6) == 1, o,
                                                        o_ref[tile_rows(qi, ch), :])
            reset(ch)
            q2_buf[ch, :] = stacked_q(q_next, ch)
            refill(0, ch)
        return carry

    lax.fori_loop(0, n_q, tile, 0)


def _fox_attention(q, qa, k, ka, v, tq):
    b, t, w = q.shape
    n_pairs = w // LANES
    slab = pl.BlockSpec((None, t, LANES), lambda i, p: (i, 0, p))
    bias_slab = pl.BlockSpec((None, t, LANES), lambda i, p: (i, 0, 0))
    return pl.pallas_call(
        functools.partial(_attn_kernel, tq=tq),
        grid=(b, n_pairs),
        in_specs=[slab, bias_slab, slab, bias_slab, slab],
        out_specs=slab,
        out_shape=jax.ShapeDtypeStruct((b, t, w), BF16),
        scratch_shapes=[pltpu.VMEM((2 * tq, 2 * LANES), BF16),
                        pltpu.VMEM((2 * tq, tq), F32),
                        pltpu.VMEM((2 * tq, LANES), F32),
                        pltpu.VMEM((2 * tq, LANES), F32),
                        pltpu.VMEM((2 * tq, 2 * LANES), F32)],
        compiler_params=_cparams("parallel", "parallel"),
        name="fox_attention",
    )(q, qa, k, ka, v)


def _pad_lanes(a):
    return jnp.pad(a, [(0, 0)] * (a.ndim - 1) + [(0, LANES - a.shape[-1])])


def kernel(x, pre_mix_norm, post_mix_norm, pre_ffn_norm, post_ffn_norm, w_ffn_in, w_ffn_out,
           gdn_w_in, gdn_conv, gdn_a_log, gdn_dt_bias, gdn_out_norm, gdn_w_out,
           kv_norm, w_kv, b_forget, fox_w_q, fox_w_o):
    b, t, d = x.shape
    n = b * t
    assert gdn_w_in.shape[0] == 1 and fox_w_q.shape[0] == 1, "one layer of each mixer"
    x2 = x.reshape(n, d)
    row = lambda a: a.reshape(1, -1).astype(F32)

    w_ffn_in_b = w_ffn_in.astype(BF16)
    w_ffn_out_b = w_ffn_out.astype(BF16)

    def mix_ffn(mix, w_mix, x2, layer):
        return _mix_ffn(mix, w_mix.astype(BF16), x2, row(post_mix_norm[layer]),
                        row(pre_ffn_norm[layer]), w_ffn_in_b, w_ffn_out_b,
                        row(post_ffn_norm[layer]), layer, ROW_TILE, FFN_SPLITS)

    qkv_w = 3 * GDN_HEADS * GDN_DK
    z_w = GDN_HEADS * GDN_DV
    w_in = gdn_w_in[0]
    w_small = _pad_lanes(w_in[:, qkv_w + z_w:]).astype(BF16)
    qkv, z, gates = _gdn_in_proj(x2, row(pre_mix_norm[0]), w_in.astype(BF16), qkv_w, z_w, w_small,
                                 gdn_conv[0].astype(F32), GDN_PROJ_ROWS, t)
    zeros_h = jnp.zeros((GDN_HEADS,), F32)
    alog_pad = _pad_lanes(jnp.concatenate([zeros_h, gdn_a_log[0].astype(F32)])[None, :])
    dtb_pad = _pad_lanes(jnp.concatenate([zeros_h, gdn_dt_bias[0].astype(F32)])[None, :])
    og = _gdn_core(qkv.reshape(b, t, qkv_w), z.reshape(b, t, z_w), gates.reshape(b, t, LANES),
                   alog_pad, dtb_pad, row(gdn_out_norm[0]), GDN_TIME_BLOCK, GDN_CHUNKS_PER_PREP)
    x2 = mix_ffn(og.reshape(n, -1), gdn_w_out[0], x2, 0)

    fw = FOX_HEADS * FOX_HD
    q, k, v, lf = _fox_proj(x2, row(pre_mix_norm[1]), row(kv_norm), fox_w_q[0].astype(BF16),
                            w_kv.astype(BF16), _pad_lanes(w_kv[:, 2 * fw:]).astype(BF16),
                            _pad_lanes(b_forget.astype(F32)[None, :]), ROW_TILE)
    qa, ka = _bias_terms(lf.reshape(b, t, LANES))
    to3 = lambda a: a.reshape(b, t, -1)
    oa = _fox_attention(to3(q), to3(qa), to3(k), to3(ka), to3(v), ATTN_TILE)
    x2 = mix_ffn(oa.reshape(n, fw), fox_w_o[0], x2, 1)
    return x2.reshape(b, t, d)
```

```python
import functools
import math

import jax
import jax.numpy as jnp
import numpy as np
from jax import lax
from jax.experimental import pallas as pl
from jax.experimental.pallas import tpu as pltpu

F32 = jnp.float32
BF16 = jnp.bfloat16

RMS_EPS = 1e-6
LANES = 128
GDN_HEADS = 8
GDN_DK = 128
GDN_DV = 128
GDN_CONV = 4
GDN_CHUNK = 64
FOX_HEADS = 16
FOX_HD = 64
NEG_BIG = -1e30
LOG2_E = math.log2(math.e)
BIAS_TERMS = 6

ROW_TILE = 512
FFN_SPLITS = 1
GDN_PROJ_ROWS = 256
GDN_PROJ_SLAB = 256
GDN_CONV_ROWS = 256
GDN_TIME_BLOCK = 512
GDN_CHUNKS_PER_PREP = 4
ATTN_TILE = 512
ATTN_ROW_CHUNK = 512

VMEM_LIMIT = 56 * 1024 * 1024


def _cparams(*sem):
    return pltpu.CompilerParams(dimension_semantics=sem, vmem_limit_bytes=VMEM_LIMIT)


def _mm(a, b):
    return jnp.dot(a.astype(BF16), b.astype(BF16), preferred_element_type=F32)


def _mm_nt(a, b):
    return lax.dot_general(a.astype(BF16), b.astype(BF16), (((1,), (1,)), ((), ())),
                           preferred_element_type=F32)


def _rms_scale(x):
    return lax.rsqrt(jnp.mean(x * x, axis=-1, keepdims=True) + RMS_EPS)


def _sigmoid(x):
    return 1.0 / (1.0 + jnp.exp2(x * (-LOG2_E)))


def _softplus(x):
    return jnp.maximum(x, 0.0) + jnp.log(1.0 + jnp.exp(-jnp.abs(x)))


def _resident(shape, col_block=0, layer=None):
    idx = (0,) * (len(shape) - 1) + (col_block,)
    if layer is not None:
        shape, idx = (None,) + tuple(shape), (layer,) + idx
    return pl.BlockSpec(shape, lambda *_: idx, pipeline_mode=pl.Buffered(1))


def _gdn_in_proj_kernel(x_ref, g_ref, wm_ref, ws_ref, cw_ref, qkv_ref, z_ref, gate_ref, tail_ref,
                        *, tiles_per_seq):
    @pl.when(pl.program_id(0) % tiles_per_seq == 0)
    def _():
        tail_ref[0:8, :] = jnp.zeros((8, tail_ref.shape[1]), F32)

    x = x_ref[...]
    tm = x.shape[0]
    h = (x * _rms_scale(x) * g_ref[...]).astype(BF16)
    qk_w = GDN_HEADS * GDN_DK
    qkv_w = qkv_ref.shape[1]
    n_slabs = qkv_w // GDN_PROJ_SLAB
    z_slabs = z_ref.shape[1] // GDN_PROJ_SLAB
    z_every = n_slabs // z_slabs
    slab = lambda j: slice(j * GDN_PROJ_SLAB, (j + 1) * GDN_PROJ_SLAB)
    proj = lambda j: jnp.dot(h, wm_ref[:, slab(j)], preferred_element_type=F32)
    for j in range(n_slabs):
        cols = slab(j)
        y = proj(j)
        if j % z_every == z_every - 1:
            z_ref[:, slab(j // z_every)] = proj(n_slabs + j // z_every)
        tail_ref[8:8 + tm, cols] = y
        w = [cw_ref[kk:kk + 1, cols] for kk in range(GDN_CONV)]
        for r0 in range(0, tm, GDN_CONV_ROWS):
            xr = tail_ref[r0:r0 + 8 + GDN_CONV_ROWS, cols]
            x1 = pltpu.roll(xr, 1, axis=0)
            c = (xr * w[3] + x1 * w[2] + pltpu.roll(xr * w[1] + x1 * w[0], 2, axis=0))[8:]
            c = c * _sigmoid(c)
            if cols.start < 2 * qk_w:
                scale = GDN_DK ** -0.5 if cols.start < qk_w else 1.0
                heads = [c[:, i:i + GDN_DK] for i in range(0, GDN_PROJ_SLAB, GDN_DK)]
                c = jnp.concatenate(
                    [hd * (lax.rsqrt(jnp.sum(hd * hd, axis=-1, keepdims=True) + RMS_EPS) * scale)
                     for hd in heads], axis=1)
            qkv_ref[r0:r0 + GDN_CONV_ROWS, cols] = c
        tail_ref[0:8, cols] = tail_ref[tm:tm + 8, cols]
    gate_ref[...] = jnp.dot(h, ws_ref[...], preferred_element_type=F32)


def _gdn_in_proj(x2, gain, w_main, qkv_w, z_w, w_small, conv_w, tm, seq_len):
    n, d = x2.shape
    ns = w_small.shape[1]
    assert seq_len % tm == 0 and qkv_w % GDN_PROJ_SLAB == 0
    row = lambda i: (i, 0)
    return pl.pallas_call(
        functools.partial(_gdn_in_proj_kernel, tiles_per_seq=seq_len // tm),
        grid=(n // tm,),
        in_specs=[pl.BlockSpec((tm, d), row), _resident((1, d)), _resident((d, qkv_w + z_w)),
                  _resident((d, ns)), _resident((GDN_CONV, qkv_w))],
        out_specs=[pl.BlockSpec((tm, qkv_w), row), pl.BlockSpec((tm, z_w), row),
                   pl.BlockSpec((tm, ns), row)],
        out_shape=[jax.ShapeDtypeStruct((n, qkv_w), F32), jax.ShapeDtypeStruct((n, z_w), F32),
                   jax.ShapeDtypeStruct((n, ns), F32)],
        scratch_shapes=[pltpu.VMEM((8 + tm, qkv_w), F32)],
        compiler_params=_cparams("arbitrary"),
        name="gdn_in_proj",
    )(x2, gain, w_main, w_small, conv_w)


def _fox_proj_kernel(x_ref, gq_ref, gkv_ref, wq_ref, wk_ref, wv_ref, wf_ref, bf_ref,
                     q_ref, k_ref, v_ref, lf_ref):
    x = x_ref[...]
    xn = x * _rms_scale(x)
    hq = (xn * gq_ref[...]).astype(BF16)
    hkv = (xn * gkv_ref[...]).astype(BF16)
    scale = FOX_HD ** -0.5 * LOG2_E
    q_ref[...] = (jnp.dot(hq, wq_ref[...], preferred_element_type=F32) * scale).astype(BF16)
    k_ref[...] = jnp.dot(hkv, wk_ref[...], preferred_element_type=F32).astype(BF16)
    v_ref[...] = jnp.dot(hkv, wv_ref[...], preferred_element_type=F32).astype(BF16)
    f = jnp.dot(hkv, wf_ref[...], preferred_element_type=F32) + bf_ref[...]
    lf_ref[...] = -_softplus(-f)


def _fox_proj(x2, gq, gkv, wq, wkv, wf, bf, tm):
    n, d = x2.shape
    w = wq.shape[1]
    row = lambda i: (i, 0)
    return pl.pallas_call(
        _fox_proj_kernel,
        grid=(n // tm,),
        in_specs=[pl.BlockSpec((tm, d), row), _resident((1, d)), _resident((1, d)),
                  _resident((d, w)), _resident((d, w), 0), _resident((d, w), 1),
                  _resident((d, LANES)), _resident((1, LANES))],
        out_specs=[pl.BlockSpec((tm, w), row), pl.BlockSpec((tm, w), row),
                   pl.BlockSpec((tm, w), row), pl.BlockSpec((tm, LANES), row)],
        out_shape=[jax.ShapeDtypeStruct((n, w), BF16), jax.ShapeDtypeStruct((n, w), BF16),
                   jax.ShapeDtypeStruct((n, w), BF16), jax.ShapeDtypeStruct((n, LANES), F32)],
        compiler_params=_cparams("parallel"),
        name="fox_proj",
    )(x2, gq, gkv, wq, wkv, wkv, wf, bf)


def _mix_ffn_kernel(a_ref, wmix_ref, x_ref, gmix_ref, g1_ref, wg_ref, wu_ref, wo_ref, g2_ref,
                    o_ref, *, ff_splits):
    y = jnp.dot(a_ref[...], wmix_ref[...], preferred_element_type=F32)
    x = x_ref[...] + y * _rms_scale(y) * gmix_ref[...]
    h = (x * _rms_scale(x) * g1_ref[...]).astype(BF16)
    step = wg_ref.shape[1] // ff_splits
    f = None
    for i in range(ff_splits):
        cols = slice(i * step, (i + 1) * step)
        g = jnp.dot(h, wg_ref[:, cols], preferred_element_type=F32)
        u = jnp.dot(h, wu_ref[:, cols], preferred_element_type=F32)
        act = (g * _sigmoid(g) * u).astype(BF16)
        part = jnp.dot(act, wo_ref[cols, :], preferred_element_type=F32)
        f = part if f is None else f + part
    o_ref[...] = x + f * _rms_scale(f) * g2_ref[...]


def _mix_ffn(a, wmix, x2, gmix, g1, w_in, wo, g2, layer, tm, ff_splits):
    n, d = x2.shape
    k = a.shape[1]
    dff = wo.shape[1]
    assert dff % (ff_splits * LANES) == 0
    row = lambda i: (i, 0)
    return pl.pallas_call(
        functools.partial(_mix_ffn_kernel, ff_splits=ff_splits),
        grid=(n // tm,),
        in_specs=[pl.BlockSpec((tm, k), row), _resident((k, d)), pl.BlockSpec((tm, d), row),
                  _resident((1, d)), _resident((1, d)), _resident((d, dff), 0, layer),
                  _resident((d, dff), 1, layer), _resident((dff, d), 0, layer), _resident((1, d))],
        out_specs=pl.BlockSpec((tm, d), row),
        out_shape=jax.ShapeDtypeStruct((n, d), F32),
        compiler_params=_cparams("parallel"),
        name="mix_ffn",
    )(a, wmix, x2, gmix, g1, w_in, w_in, wo, g2)


def _gdn_kernel(qkv_ref, z_ref, gate_ref, alog_ref, dtb_ref, onorm_ref, o_ref,
                s_ref, beta_s, gcum_s, wq_s, u_s, aqk_s, kdt_s, gl_s, *, tb, chunks_per_prep):
    c = GDN_CHUNK
    n_chunks = tb // c
    nh = GDN_HEADS
    k_off = nh * GDN_DK
    v_off = 2 * nh * GDN_DK

    @pl.when(pl.program_id(1) == 0)
    def _():
        s_ref[...] = jnp.zeros_like(s_ref)

    gates = gate_ref[...]
    beta_s[...] = _sigmoid(gates)
    g = -jnp.exp(alog_ref[...]) * _softplus(gates + dtb_ref[...])
    row = lax.broadcasted_iota(jnp.int32, (tb, LANES), 0) & (c - 1)
    sh = 1
    while sh < c:
        g = g + jnp.where(row >= sh, pltpu.roll(g, sh, axis=0), 0.0)
        sh *= 2
    gcum_s[...] = g

    ri = lax.broadcasted_iota(jnp.int32, (c, c), 0)
    ci = lax.broadcasted_iota(jnp.int32, (c, c), 1)
    causal = ri >= ci
    strict = ri > ci
    eye = (ri == ci).astype(F32)
    level_masks = []
    s = 1
    while s < c:
        blk = ~(2 * s - 1)
        m = ((ri & blk) == (ci & blk)) & ((ri & (2 * s - 1)) >= s) & ((ci & (2 * s - 1)) < s)
        level_masks.append(m.astype(F32))
        s *= 2

    def prep_body(ip, carry):
        elems = []
        per_chunk = []
        for cc in range(chunks_per_prep):
            ic = ip * chunks_per_prep + cc
            rows = pl.ds(pl.multiple_of(ic * c, c), c)
            bt = beta_s[rows, :]
            gc = gcum_s[rows, :]
            g_last = gc[c - 1:c, :]
            per_chunk.append(dict(
                bt=bt, gc=gc, e=jnp.exp(gc), dk=jnp.exp(g_last - gc), gl=jnp.exp(g_last),
                gc_t=jnp.transpose(jnp.concatenate([gc, jnp.zeros_like(gc)], axis=0))))
            for h in range(nh):
                elems.append((cc, ic, rows, h))

        def col(name, cc, lane):
            return per_chunk[cc][name][:, lane:lane + 1]

        hcols = lambda h, off=0: slice(off + h * LANES, off + (h + 1) * LANES)
        kb = [qkv_ref[rows, hcols(h, k_off)].astype(BF16) for (_, _, rows, h) in elems]
        kk_m = [_mm_nt(x, x) for x in kb]
        decay = []
        for (cc, _, _, h) in elems:
            diff = col("gc", cc, nh + h) - per_chunk[cc]["gc_t"][nh + h:nh + h + 1, :c]
            decay.append(jnp.where(causal, jnp.exp(jnp.where(causal, diff, 0.0)), 0.0))
        a_strict = [jnp.where(strict, col("bt", cc, h) * kk_m[i] * decay[i], 0.0)
                    for i, (cc, _, _, h) in enumerate(elems)]
        d = [eye - a * level_masks[0] for a in a_strict]
        for m in level_masks[1:]:
            x = [_mm(a * m, di) for a, di in zip(a_strict, d)]
            d = [di - _mm(di, xi) for di, xi in zip(d, x)]
        for i, (cc, ic, rows, h) in enumerate(elems):
            kh = qkv_ref[rows, hcols(h, k_off)]
            b_col = col("bt", cc, h)
            rhs = jnp.concatenate([(b_col * col("e", cc, nh + h)) * kh,
                                   b_col * qkv_ref[rows, hcols(h, v_off)]], axis=1)
            sol = _mm(d[i], rhs)
            qh = qkv_ref[rows, hcols(h)]
            wq_s[ic, h, :c, :] = sol[:, :GDN_DK].astype(BF16)
            wq_s[ic, h, c:, :] = (qh * col("e", cc, nh + h)).astype(BF16)
            u_s[ic, h] = sol[:, GDN_DK:]
            aqk_s[ic, h] = (_mm_nt(qh, kb[i]) * decay[i]).astype(BF16)
            kdt_s[ic, h] = jnp.transpose(kh * col("dk", cc, nh + h)).astype(BF16)
            gl_s[ic, h] = jnp.broadcast_to(per_chunk[cc]["gl"][:, nh + h:nh + h + 1], (1, LANES))
        return carry

    lax.fori_loop(0, n_chunks // chunks_per_prep, prep_body, 0)

    def scan_body(ic, carry):
        rows = pl.ds(pl.multiple_of(ic * c, c), c)
        st = [s_ref[h] for h in range(nh)]
        wq = [_mm(wq_s[ic, h], st[h]) for h in range(nh)]
        v_new = [u_s[ic, h] - wq[h][:c] for h in range(nh)]
        v_new_b = [x.astype(BF16) for x in v_new]
        o = [wq[h][c:] + jnp.dot(aqk_s[ic, h], v_new_b[h], preferred_element_type=F32)
             for h in range(nh)]
        for h in range(nh):
            s_ref[h] = st[h] * gl_s[ic, h] + jnp.dot(kdt_s[ic, h], v_new_b[h],
                                                    preferred_element_type=F32)
        for h in range(nh):
            cols = slice(h * LANES, (h + 1) * LANES)
            zh = z_ref[rows, cols]
            on = o[h] * _rms_scale(o[h]) * onorm_ref[...]
            o_ref[rows, cols] = (on * (zh * _sigmoid(zh))).astype(o_ref.dtype)
        return carry

    lax.fori_loop(0, n_chunks, scan_body, 0)


def _gdn_core(qkv, z, gates, alog_pad, dtb_pad, out_norm, tb, chunks_per_prep):
    b, t, qkv_w = qkv.shape
    v_w = z.shape[2]
    c = GDN_CHUNK
    nc = tb // c
    blk = lambda w: pl.BlockSpec((None, tb, w), lambda i, j: (i, j, 0))
    return pl.pallas_call(
        functools.partial(_gdn_kernel, tb=tb, chunks_per_prep=chunks_per_prep),
        grid=(b, t // tb),
        in_specs=[blk(qkv_w), blk(v_w), blk(LANES),
                  _resident((1, LANES)), _resident((1, LANES)), _resident((1, GDN_DV))],
        out_specs=blk(v_w),
        out_shape=jax.ShapeDtypeStruct((b, t, v_w), BF16),
        scratch_shapes=[pltpu.VMEM((GDN_HEADS, GDN_DK, GDN_DV), F32),
                        pltpu.VMEM((tb, LANES), F32),
                        pltpu.VMEM((tb, LANES), F32),
                        pltpu.VMEM((nc, GDN_HEADS, 2 * c, GDN_DK), BF16),
                        pltpu.VMEM((nc, GDN_HEADS, c, GDN_DV), F32),
                        pltpu.VMEM((nc, GDN_HEADS, c, c), BF16),
                        pltpu.VMEM((nc, GDN_HEADS, GDN_DK, c), BF16),
                        pltpu.VMEM((nc, GDN_HEADS, 1, LANES), F32)],
        compiler_params=_cparams("parallel", "arbitrary"),
        name="gdn_core",
    )(qkv, z, gates, alog_pad, dtb_pad, out_norm)


def _bias_terms_kernel(x_ref, selq_ref, selk_ref, oneq_ref, onek_ref, qa_ref, ka_ref):
    x = x_ref[...]
    t = x.shape[0]
    row = lax.broadcasted_iota(jnp.int32, x.shape, 0)
    sh = 1
    while sh < t:
        x = x + jnp.where(row >= sh, pltpu.roll(x, sh, axis=0), 0.0)
        sh *= 2
    c = x * LOG2_E
    hi = c.astype(BF16)
    r = c - hi.astype(F32)
    mid = r.astype(BF16)
    lo = (r - mid.astype(F32)).astype(BF16)
    parts = (hi, mid, lo)
    qa = oneq_ref[...]
    ka = onek_ref[...]
    for i in range(3):
        qa = qa + jnp.dot(parts[i], selq_ref[i], preferred_element_type=F32)
        ka = ka - jnp.dot(parts[i], selk_ref[i], preferred_element_type=F32)
    qa_ref[...] = qa.astype(BF16)
    ka_ref[...] = ka.astype(BF16)


def _bias_layout():
    w = LANES
    assert FOX_HEADS * BIAS_TERMS <= w
    selq = np.zeros((3, LANES, w), np.float32)
    selk = np.zeros((3, LANES, w), np.float32)
    oneq = np.zeros((1, w), np.float32)
    onek = np.zeros((1, w), np.float32)
    for h in range(FOX_HEADS):
        base = BIAS_TERMS * h
        for i in range(3):
            selq[i, h, base + i] = 1.0
            selk[i, h, base + 3 + i] = 1.0
            oneq[0, base + 3 + i] = 1.0
            onek[0, base + i] = 1.0
    return (jnp.asarray(selq, BF16), jnp.asarray(selk, BF16), jnp.asarray(oneq), jnp.asarray(onek))


def _bias_terms(lf):
    b, t, w = lf.shape
    selq, selk, oneq, onek = _bias_layout()
    seq = pl.BlockSpec((None, t, w), lambda i: (i, 0, 0))
    return pl.pallas_call(
        _bias_terms_kernel,
        grid=(b,),
        in_specs=[seq, _resident((3, LANES, w)), _resident((3, LANES, w)),
                  _resident((1, w)), _resident((1, w))],
        out_specs=[seq, seq],
        out_shape=[jax.ShapeDtypeStruct((b, t, w), BF16), jax.ShapeDtypeStruct((b, t, w), BF16)],
        compiler_params=_cparams("parallel"),
        name="forget_bias_terms",
    )(lf, selq, selk, oneq, onek)


def _attn_kernel(q_ref, qa_ref, k_ref, ka_ref, v_ref, o_ref, q2_buf, s_buf, r_buf, m_buf, acc_buf,
                 *, tq):
    n_q = q_ref.shape[0] // tq
    lane = lax.broadcasted_iota(jnp.int32, (1, LANES), 1)
    ones = jnp.ones((tq, LANES), BF16)
    rc = ATTN_ROW_CHUNK
    assert tq % rc == 0, "a row chunk must stay inside one head of the stacked tile"
    chunks = [slice(r0, r0 + rc) for r0 in range(0, 2 * tq, rc)]
    n_tiles = tq // LANES
    ri = lax.broadcasted_iota(jnp.int32, (rc, tq), 0)
    ci = lax.broadcasted_iota(jnp.int32, (rc, tq), 1)

    def key_rows(j):
        return pl.ds(pl.multiple_of(j * tq, tq), tq)

    def tile_rows(qi, ch):
        return pl.ds(pl.multiple_of(qi * tq + (ch.start & (tq - 1)), rc), rc)

    def stacked_q(qi, ch):
        p = ch.start // tq
        q = q_ref[tile_rows(qi, ch), :]
        qa = qa_ref[tile_rows(qi, ch), :]
        head = (lane >> 6) == p
        b0 = BIAS_TERMS * (2 * pl.program_id(1) + p)
        bias = (lane >= b0) & (lane < b0 + BIAS_TERMS)
        zero = jnp.zeros_like(q)
        return jnp.concatenate([jnp.where(head, q, zero), jnp.where(bias, qa, zero)], axis=1)

    def logits(rows, ch):
        kf = jnp.concatenate([k_ref[rows, :], ka_ref[rows, :]], axis=1)
        return lax.dot_general(q2_buf[ch, :], kf, (((1,), (1,)), ((), ())),
                               preferred_element_type=F32)

    def row_max(s):
        return jnp.broadcast_to(jnp.max(s, axis=-1, keepdims=True), (s.shape[0], LANES))

    def absorb(ch, rows):
        vf = jnp.concatenate([v_ref[rows, :], ones], axis=1)
        s, m = s_buf[ch, :], m_buf[ch, :]
        m_new = jnp.maximum(m, r_buf[ch, :])
        alpha = jnp.exp2(m - m_new)
        pm = jnp.concatenate([jnp.exp2(s[:, i * LANES:(i + 1) * LANES] - m_new).astype(BF16)
                              for i in range(n_tiles)], axis=1)
        pv = jnp.dot(pm, vf, preferred_element_type=F32)
        return m_new, jnp.concatenate([alpha, alpha], axis=1) * acc_buf[ch, :] + pv

    def refill(j, ch):
        sn = logits(key_rows(j), ch)
        s_buf[ch, :] = sn
        r_buf[ch, :] = row_max(sn)

    def reset(ch):
        m_buf[ch, :] = jnp.full((rc, LANES), NEG_BIG, F32)
        acc_buf[ch, :] = jnp.zeros((rc, 2 * LANES), F32)

    for ch in chunks:
        q2_buf[ch, :] = stacked_q(0, ch)
        reset(ch)
        refill(0, ch)

    def tile(qi, carry):
        def step(j):
            for ch in chunks:
                m_new, acc_new = absorb(ch, key_rows(j))
                m_buf[ch, :] = m_new
                acc_buf[ch, :] = acc_new
                refill(j + 1, ch)

        def quad_step(i, c):
            for u in range(4):
                step(4 * i + u)
            return c

        lax.fori_loop(0, qi >> 2, quad_step, 0)
        done = qi & ~3

        @pl.when((qi & 2) != 0)
        def _():
            step(done)
            step(done + 1)

        @pl.when((qi & 1) != 0)
        def _():
            step(qi - 1)

        for ch in chunks:
            s = jnp.where(ri + (ch.start & (tq - 1)) >= ci, s_buf[ch, :], NEG_BIG)
            s_buf[ch, :] = s
            r_buf[ch, :] = row_max(s)
        q_next = jnp.minimum(qi + 1, n_q - 1)
        for ch in chunks:
            _, acc = absorb(ch, key_rows(qi))
            o = (acc[:, :LANES] / acc[:, LANES:]).astype(o_ref.dtype)
            if ch.start < tq:
                o_ref[tile_rows(qi, ch), :] = o
            else:
                o_ref[tile_rows(qi, ch), :] = jnp.where((lane >> 6) == 1, o,
                                                        o_ref[tile_rows(qi, ch), :])
            reset(ch)
            q2_buf[ch, :] = stacked_q(q_next, ch)
            refill(0, ch)
        return carry

    lax.fori_loop(0, n_q, tile, 0)


def _fox_attention(q, qa, k, ka, v, tq):
    b, t, w = q.shape
    n_pairs = w // LANES
    slab = pl.BlockSpec((None, t, LANES), lambda i, p: (i, 0, p))
    bias_slab = pl.BlockSpec((None, t, LANES), lambda i, p: (i, 0, 0))
    return pl.pallas_call(
        functools.partial(_attn_kernel, tq=tq),
        grid=(b, n_pairs),
        in_specs=[slab, bias_slab, slab, bias_slab, slab],
        out_specs=slab,
        out_shape=jax.ShapeDtypeStruct((b, t, w), BF16),
        scratch_shapes=[pltpu.VMEM((2 * tq, 2 * LANES), BF16),
                        pltpu.VMEM((2 * tq, tq), F32),
                        pltpu.VMEM((2 * tq, LANES), F32),
                        pltpu.VMEM((2 * tq, LANES), F32),
                        pltpu.VMEM((2 * tq, 2 * LANES), F32)],
        compiler_params=_cparams("parallel", "parallel"),
        name="fox_attention",
    )(q, qa, k, ka, v)


def _pad_lanes(a):
    return jnp.pad(a, [(0, 0)] * (a.ndim - 1) + [(0, LANES - a.shape[-1])])


def kernel(x, pre_mix_norm, post_mix_norm, pre_ffn_norm, post_ffn_norm, w_ffn_in, w_ffn_out,
           gdn_w_in, gdn_conv, gdn_a_log, gdn_dt_bias, gdn_out_norm, gdn_w_out,
           kv_norm, w_kv, b_forget, fox_w_q, fox_w_o):
    b, t, d = x.shape
    n = b * t
    assert gdn_w_in.shape[0] == 1 and fox_w_q.shape[0] == 1, "one layer of each mixer"
    x2 = x.reshape(n, d)
    row = lambda a: a.reshape(1, -1).astype(F32)

    w_ffn_in_b = w_ffn_in.astype(BF16)
    w_ffn_out_b = w_ffn_out.astype(BF16)

    def mix_ffn(mix, w_mix, x2, layer):
        return _mix_ffn(mix, w_mix.astype(BF16), x2, row(post_mix_norm[layer]),
                        row(pre_ffn_norm[layer]), w_ffn_in_b, w_ffn_out_b,
                        row(post_ffn_norm[layer]), layer, ROW_TILE, FFN_SPLITS)

    qkv_w = 3 * GDN_HEADS * GDN_DK
    z_w = GDN_HEADS * GDN_DV
    w_in = gdn_w_in[0]
    w_small = _pad_lanes(w_in[:, qkv_w + z_w:]).astype(BF16)
    qkv, z, gates = _gdn_in_proj(x2, row(pre_mix_norm[0]), w_in.astype(BF16), qkv_w, z_w, w_small,
                                 gdn_conv[0].astype(F32), GDN_PROJ_ROWS, t)
    zeros_h = jnp.zeros((GDN_HEADS,), F32)
    alog_pad = _pad_lanes(jnp.concatenate([zeros_h, gdn_a_log[0].astype(F32)])[None, :])
    dtb_pad = _pad_lanes(jnp.concatenate([zeros_h, gdn_dt_bias[0].astype(F32)])[None, :])
    og = _gdn_core(qkv.reshape(b, t, qkv_w), z.reshape(b, t, z_w), gates.reshape(b, t, LANES),
                   alog_pad, dtb_pad, row(gdn_out_norm[0]), GDN_TIME_BLOCK, GDN_CHUNKS_PER_PREP)
    x2 = mix_ffn(og.reshape(n, -1), gdn_w_out[0], x2, 0)

    fw = FOX_HEADS * FOX_HD
    q, k, v, lf = _fox_proj(x2, row(pre_mix_norm[1]), row(kv_norm), fox_w_q[0].astype(BF16),
                            w_kv.astype(BF16), _pad_lanes(w_kv[:, 2 * fw:]).astype(BF16),
                            _pad_lanes(b_forget.astype(F32)[None, :]), ROW_TILE)
    qa, ka = _bias_terms(lf.reshape(b, t, LANES))
    to3 = lambda a: a.reshape(b, t, -1)
    oa = _fox_attention(to3(q), to3(qa), to3(k), to3(ka), to3(v), ATTN_TILE)
    x2 = mix_ffn(oa.reshape(n, fw), fox_w_o[0], x2, 1)
    return x2.reshape(b, t, d)
```

```python
import functools
import math

import jax
import jax.numpy as jnp
import numpy as np
from jax import lax
from jax.experimental import pallas as pl
from jax.experimental.pallas import tpu as pltpu

F32 = jnp.float32
BF16 = jnp.bfloat16

RMS_EPS = 1e-6
LANES = 128
GDN_HEADS = 8
GDN_DK = 128
GDN_DV = 128
GDN_CONV = 4
GDN_CHUNK = 64
FOX_HEADS = 16
FOX_HD = 64
NEG_BIG = -1e30
LOG2_E = math.log2(math.e)
BIAS_TERMS = 6

ROW_TILE = 512
FFN_SPLITS = 1
GDN_PROJ_ROWS = 512
GDN_PROJ_SLAB = 256
GDN_CONV_ROWS = 256
GDN_TIME_BLOCK = 256
GDN_CHUNKS_PER_PREP = 4
ATTN_TILE = 512
ATTN_ROW_CHUNK = 512

VMEM_LIMIT = 56 * 1024 * 1024


def _cparams(*sem):
    return pltpu.CompilerParams(dimension_semantics=sem, vmem_limit_bytes=VMEM_LIMIT)


def _mm(a, b):
    return jnp.dot(a.astype(BF16), b.astype(BF16), preferred_element_type=F32)


def _mm_nt(a, b):
    return lax.dot_general(a.astype(BF16), b.astype(BF16), (((1,), (1,)), ((), ())),
                           preferred_element_type=F32)


def _rms_scale(x):
    return lax.rsqrt(jnp.mean(x * x, axis=-1, keepdims=True) + RMS_EPS)


def _sigmoid(x):
    return 1.0 / (1.0 + jnp.exp2(x * (-LOG2_E)))


def _softplus(x):
    return jnp.maximum(x, 0.0) + jnp.log(1.0 + jnp.exp(-jnp.abs(x)))


def _resident(shape, col_block=0, layer=None):
    idx = (0,) * (len(shape) - 1) + (col_block,)
    if layer is not None:
        shape, idx = (None,) + tuple(shape), (layer,) + idx
    return pl.BlockSpec(shape, lambda *_: idx, pipeline_mode=pl.Buffered(1))


def _gdn_in_proj_kernel(x_ref, g_ref, wm_ref, ws_ref, cw_ref, qkv_ref, z_ref, gate_ref, tail_ref,
                        *, tiles_per_seq):
    @pl.when(pl.program_id(0) % tiles_per_seq == 0)
    def _():
        tail_ref[0:8, :] = jnp.zeros((8, tail_ref.shape[1]), F32)

    x = x_ref[...]
    tm = x.shape[0]
    h = (x * _rms_scale(x) * g_ref[...]).astype(BF16)
    qk_w = GDN_HEADS * GDN_DK
    qkv_w = qkv_ref.shape[1]
    n_slabs = qkv_w // GDN_PROJ_SLAB
    z_slabs = z_ref.shape[1] // GDN_PROJ_SLAB
    z_every = n_slabs // z_slabs
    slab = lambda j: slice(j * GDN_PROJ_SLAB, (j + 1) * GDN_PROJ_SLAB)
    proj = lambda j: jnp.dot(h, wm_ref[:, slab(j)], preferred_element_type=F32)
    for j in range(n_slabs):
        cols = slab(j)
        y = proj(j)
        if j % z_every == z_every - 1:
            z_ref[:, slab(j // z_every)] = proj(n_slabs + j // z_every)
        tail_ref[8:8 + tm, cols] = y
        w = [cw_ref[kk:kk + 1, cols] for kk in range(GDN_CONV)]
        for r0 in range(0, tm, GDN_CONV_ROWS):
            xr = tail_ref[r0:r0 + 8 + GDN_CONV_ROWS, cols]
            x1 = pltpu.roll(xr, 1, axis=0)
            c = (xr * w[3] + x1 * w[2] + pltpu.roll(xr * w[1] + x1 * w[0], 2, axis=0))[8:]
            c = c * _sigmoid(c)
            if cols.start < 2 * qk_w:
                scale = GDN_DK ** -0.5 if cols.start < qk_w else 1.0
                heads = [c[:, i:i + GDN_DK] for i in range(0, GDN_PROJ_SLAB, GDN_DK)]
                c = jnp.concatenate(
                    [hd * (lax.rsqrt(jnp.sum(hd * hd, axis=-1, keepdims=True) + RMS_EPS) * scale)
                     for hd in heads], axis=1)
            qkv_ref[r0:r0 + GDN_CONV_ROWS, cols] = c
        tail_ref[0:8, cols] = tail_ref[tm:tm + 8, cols]
    gate_ref[...] = jnp.dot(h, ws_ref[...], preferred_element_type=F32)


def _gdn_in_proj(x2, gain, w_main, qkv_w, z_w, w_small, conv_w, tm, seq_len):
    n, d = x2.shape
    ns = w_small.shape[1]
    assert seq_len % tm == 0 and qkv_w % GDN_PROJ_SLAB == 0
    row = lambda i: (i, 0)
    return pl.pallas_call(
        functools.partial(_gdn_in_proj_kernel, tiles_per_seq=seq_len // tm),
        grid=(n // tm,),
        in_specs=[pl.BlockSpec((tm, d), row), _resident((1, d)), _resident((d, qkv_w + z_w)),
                  _resident((d, ns)), _resident((GDN_CONV, qkv_w))],
        out_specs=[pl.BlockSpec((tm, qkv_w), row), pl.BlockSpec((tm, z_w), row),
                   pl.BlockSpec((tm, ns), row)],
        out_shape=[jax.ShapeDtypeStruct((n, qkv_w), F32), jax.ShapeDtypeStruct((n, z_w), F32),
                   jax.ShapeDtypeStruct((n, ns), F32)],
        scratch_shapes=[pltpu.VMEM((8 + tm, qkv_w), F32)],
        compiler_params=_cparams("arbitrary"),
        name="gdn_in_proj",
    )(x2, gain, w_main, w_small, conv_w)


def _fox_proj_kernel(x_ref, gq_ref, gkv_ref, wq_ref, wk_ref, wv_ref, wf_ref, bf_ref,
                     q_ref, k_ref, v_ref, lf_ref):
    x = x_ref[...]
    xn = x * _rms_scale(x)
    hq = (xn * gq_ref[...]).astype(BF16)
    hkv = (xn * gkv_ref[...]).astype(BF16)
    scale = FOX_HD ** -0.5 * LOG2_E
    q_ref[...] = (jnp.dot(hq, wq_ref[...], preferred_element_type=F32) * scale).astype(BF16)
    k_ref[...] = jnp.dot(hkv, wk_ref[...], preferred_element_type=F32).astype(BF16)
    v_ref[...] = jnp.dot(hkv, wv_ref[...], preferred_element_type=F32).astype(BF16)
    f = jnp.dot(hkv, wf_ref[...], preferred_element_type=F32) + bf_ref[...]
    lf_ref[...] = -_softplus(-f)


def _fox_proj(x2, gq, gkv, wq, wkv, wf, bf, tm):
    n, d = x2.shape
    w = wq.shape[1]
    row = lambda i: (i, 0)
    return pl.pallas_call(
        _fox_proj_kernel,
        grid=(n // tm,),
        in_specs=[pl.BlockSpec((tm, d), row), _resident((1, d)), _resident((1, d)),
                  _resident((d, w)), _resident((d, w), 0), _resident((d, w), 1),
                  _resident((d, LANES)), _resident((1, LANES))],
        out_specs=[pl.BlockSpec((tm, w), row), pl.BlockSpec((tm, w), row),
                   pl.BlockSpec((tm, w), row), pl.BlockSpec((tm, LANES), row)],
        out_shape=[jax.ShapeDtypeStruct((n, w), BF16), jax.ShapeDtypeStruct((n, w), BF16),
                   jax.ShapeDtypeStruct((n, w), BF16), jax.ShapeDtypeStruct((n, LANES), F32)],
        compiler_params=_cparams("parallel"),
        name="fox_proj",
    )(x2, gq, gkv, wq, wkv, wkv, wf, bf)


def _mix_ffn_kernel(a_ref, wmix_ref, x_ref, gmix_ref, g1_ref, wg_ref, wu_ref, wo_ref, g2_ref,
                    o_ref, *, ff_splits):
    y = jnp.dot(a_ref[...], wmix_ref[...], preferred_element_type=F32)
    x = x_ref[...] + y * _rms_scale(y) * gmix_ref[...]
    h = (x * _rms_scale(x) * g1_ref[...]).astype(BF16)
    step = wg_ref.shape[1] // ff_splits
    f = None
    for i in range(ff_splits):
        cols = slice(i * step, (i + 1) * step)
        g = jnp.dot(h, wg_ref[:, cols], preferred_element_type=F32)
        u = jnp.dot(h, wu_ref[:, cols], preferred_element_type=F32)
        act = (g * _sigmoid(g) * u).astype(BF16)
        part = jnp.dot(act, wo_ref[cols, :], preferred_element_type=F32)
        f = part if f is None else f + part
    o_ref[...] = x + f * _rms_scale(f) * g2_ref[...]


def _mix_ffn(a, wmix, x2, gmix, g1, w_in, wo, g2, layer, tm, ff_splits):
    n, d = x2.shape
    k = a.shape[1]
    dff = wo.shape[1]
    assert dff % (ff_splits * LANES) == 0
    row = lambda i: (i, 0)
    return pl.pallas_call(
        functools.partial(_mix_ffn_kernel, ff_splits=ff_splits),
        grid=(n // tm,),
        in_specs=[pl.BlockSpec((tm, k), row), _resident((k, d)), pl.BlockSpec((tm, d), row),
                  _resident((1, d)), _resident((1, d)), _resident((d, dff), 0, layer),
                  _resident((d, dff), 1, layer), _resident((dff, d), 0, layer), _resident((1, d))],
        out_specs=pl.BlockSpec((tm, d), row),
        out_shape=jax.ShapeDtypeStruct((n, d), F32),
        compiler_params=_cparams("parallel"),
        name="mix_ffn",
    )(a, wmix, x2, gmix, g1, w_in, w_in, wo, g2)


def _gdn_kernel(qkv_ref, z_ref, gate_ref, alog_ref, dtb_ref, onorm_ref, o_ref,
                s_ref, beta_s, gcum_s, wq_s, u_s, aqk_s, kdt_s, gl_s, *, tb, chunks_per_prep):
    c = GDN_CHUNK
    n_chunks = tb // c
    nh = GDN_HEADS
    k_off = nh * GDN_DK
    v_off = 2 * nh * GDN_DK

    @pl.when(pl.program_id(1) == 0)
    def _():
        s_ref[...] = jnp.zeros_like(s_ref)

    gates = gate_ref[...]
    beta_s[...] = _sigmoid(gates)
    g = -jnp.exp(alog_ref[...]) * _softplus(gates + dtb_ref[...])
    row = lax.broadcasted_iota(jnp.int32, (tb, LANES), 0) & (c - 1)
    sh = 1
    while sh < c:
        g = g + jnp.where(row >= sh, pltpu.roll(g, sh, axis=0), 0.0)
        sh *= 2
    gcum_s[...] = g

    ri = lax.broadcasted_iota(jnp.int32, (c, c), 0)
    ci = lax.broadcasted_iota(jnp.int32, (c, c), 1)
    causal = ri >= ci
    strict = ri > ci
    eye = (ri == ci).astype(F32)
    diag8 = (strict & ((ri & ~7) == (ci & ~7))).astype(F32)
    level_masks = []
    s = 1
    while s < c:
        blk = ~(2 * s - 1)
        m = ((ri & blk) == (ci & blk)) & ((ri & (2 * s - 1)) >= s) & ((ci & (2 * s - 1)) < s)
        level_masks.append(m.astype(F32))
        s *= 2

    def prep_body(ip, carry):
        elems = []
        per_chunk = []
        for cc in range(chunks_per_prep):
            ic = ip * chunks_per_prep + cc
            rows = pl.ds(pl.multiple_of(ic * c, c), c)
            bt = beta_s[rows, :]
            gc = gcum_s[rows, :]
            g_last = gc[c - 1:c, :]
            per_chunk.append(dict(
                bt=bt, gc=gc, e=jnp.exp(gc), dk=jnp.exp(g_last - gc), gl=jnp.exp(g_last),
                gc_t=jnp.transpose(jnp.concatenate([gc, jnp.zeros_like(gc)], axis=0))))
            for h in range(nh):
                elems.append((cc, ic, rows, h))

        def col(name, cc, lane):
            return per_chunk[cc][name][:, lane:lane + 1]

        hcols = lambda h, off=0: slice(off + h * LANES, off + (h + 1) * LANES)
        kb = [qkv_ref[rows, hcols(h, k_off)].astype(BF16) for (_, _, rows, h) in elems]
        kk_m = [_mm_nt(x, x) for x in kb]
        decay = []
        for (cc, _, _, h) in elems:
            diff = col("gc", cc, nh + h) - per_chunk[cc]["gc_t"][nh + h:nh + h + 1, :c]
            decay.append(jnp.where(causal, jnp.exp(jnp.where(causal, diff, 0.0)), 0.0))
        a_strict = [jnp.where(strict, col("bt", cc, h) * kk_m[i] * decay[i], 0.0)
                    for i, (cc, _, _, h) in enumerate(elems)]
        n8 = [a * diag8 for a in a_strict]
        n2 = [_mm(x, x) for x in n8]
        q = [(eye - x) + _mm(eye - x, y) for x, y in zip(n8, n2)]
        n4 = [_mm(y, y) for y in n2]
        d = [qi + _mm(qi, z) for qi, z in zip(q, n4)]
        for m in level_masks[3:]:
            x = [_mm(a * m, di) for a, di in zip(a_strict, d)]
            d = [di - _mm(di, xi) for di, xi in zip(d, x)]
        for i, (cc, ic, rows, h) in enumerate(elems):
            kh = qkv_ref[rows, hcols(h, k_off)]
            b_col = col("bt", cc, h)
            rhs = jnp.concatenate([(b_col * col("e", cc, nh + h)) * kh,
                                   b_col * qkv_ref[rows, hcols(h, v_off)]], axis=1)
            sol = _mm(d[i], rhs)
            qh = qkv_ref[rows, hcols(h)]
            wq_s[ic, h, :c, :] = sol[:, :GDN_DK].astype(BF16)
            wq_s[ic, h, c:, :] = (qh * col("e", cc, nh + h)).astype(BF16)
            u_s[ic, h] = sol[:, GDN_DK:]
            aqk_s[ic, h] = (_mm_nt(qh, kb[i]) * decay[i]).astype(BF16)
            kdt_s[ic, h] = jnp.transpose(kh * col("dk", cc, nh + h)).astype(BF16)
            gl_s[ic, h] = jnp.broadcast_to(per_chunk[cc]["gl"][:, nh + h:nh + h + 1], (1, LANES))
        return carry

    lax.fori_loop(0, n_chunks // chunks_per_prep, prep_body, 0)

    def scan_body(ic, carry):
        rows = pl.ds(pl.multiple_of(ic * c, c), c)
        st = [s_ref[h] for h in range(nh)]
        wq = [_mm(wq_s[ic, h], st[h]) for h in range(nh)]
        v_new = [u_s[ic, h] - wq[h][:c] for h in range(nh)]
        v_new_b = [x.astype(BF16) for x in v_new]
        o = [wq[h][c:] + jnp.dot(aqk_s[ic, h], v_new_b[h], preferred_element_type=F32)
             for h in range(nh)]
        for h in range(nh):
            s_ref[h] = st[h] * gl_s[ic, h] + jnp.dot(kdt_s[ic, h], v_new_b[h],
                                                    preferred_element_type=F32)
        for h in range(nh):
            cols = slice(h * LANES, (h + 1) * LANES)
            zh = z_ref[rows, cols]
            on = o[h] * _rms_scale(o[h]) * onorm_ref[...]
            o_ref[rows, cols] = (on * (zh * _sigmoid(zh))).astype(o_ref.dtype)
        return carry

    lax.fori_loop(0, n_chunks, scan_body, 0)


def _gdn_core(qkv, z, gates, alog_pad, dtb_pad, out_norm, tb, chunks_per_prep):
    b, t, qkv_w = qkv.shape
    v_w = z.shape[2]
    c = GDN_CHUNK
    nc = tb // c
    blk = lambda w: pl.BlockSpec((None, tb, w), lambda i, j: (i, j, 0))
    return pl.pallas_call(
        functools.partial(_gdn_kernel, tb=tb, chunks_per_prep=chunks_per_prep),
        grid=(b, t // tb),
        in_specs=[blk(qkv_w), blk(v_w), blk(LANES),
                  _resident((1, LANES)), _resident((1, LANES)), _resident((1, GDN_DV))],
        out_specs=blk(v_w),
        out_shape=jax.ShapeDtypeStruct((b, t, v_w), BF16),
        scratch_shapes=[pltpu.VMEM((GDN_HEADS, GDN_DK, GDN_DV), F32),
                        pltpu.VMEM((tb, LANES), F32),
                        pltpu.VMEM((tb, LANES), F32),
                        pltpu.VMEM((nc, GDN_HEADS, 2 * c, GDN_DK), BF16),
                        pltpu.VMEM((nc, GDN_HEADS, c, GDN_DV), F32),
                        pltpu.VMEM((nc, GDN_HEADS, c, c), BF16),
                        pltpu.VMEM((nc, GDN_HEADS, GDN_DK, c), BF16),
                        pltpu.VMEM((nc, GDN_HEADS, 1, LANES), F32)],
        compiler_params=_cparams("parallel", "arbitrary"),
        name="gdn_core",
    )(qkv, z, gates, alog_pad, dtb_pad, out_norm)


def _bias_terms_kernel(x_ref, selq_ref, selk_ref, oneq_ref, onek_ref, qa_ref, ka_ref):
    x = x_ref[...]
    t = x.shape[0]
    row = lax.broadcasted_iota(jnp.int32, x.shape, 0)
    sh = 1
    while sh < t:
        x = x + jnp.where(row >= sh, pltpu.roll(x, sh, axis=0), 0.0)
        sh *= 2
    c = x * LOG2_E
    hi = c.astype(BF16)
    r = c - hi.astype(F32)
    mid = r.astype(BF16)
    lo = (r - mid.astype(F32)).astype(BF16)
    parts = (hi, mid, lo)
    qa = oneq_ref[...]
    ka = onek_ref[...]
    for i in range(3):
        qa = qa + jnp.dot(parts[i], selq_ref[i], preferred_element_type=F32)
        ka = ka - jnp.dot(parts[i], selk_ref[i], preferred_element_type=F32)
    qa_ref[...] = qa.astype(BF16)
    ka_ref[...] = ka.astype(BF16)


def _bias_layout():
    w = LANES
    assert FOX_HEADS * BIAS_TERMS <= w
    selq = np.zeros((3, LANES, w), np.float32)
    selk = np.zeros((3, LANES, w), np.float32)
    oneq = np.zeros((1, w), np.float32)
    onek = np.zeros((1, w), np.float32)
    for h in range(FOX_HEADS):
        base = BIAS_TERMS * h
        for i in range(3):
            selq[i, h, base + i] = 1.0
            selk[i, h, base + 3 + i] = 1.0
            oneq[0, base + 3 + i] = 1.0
            onek[0, base + i] = 1.0
    return (jnp.asarray(selq, BF16), jnp.asarray(selk, BF16), jnp.asarray(oneq), jnp.asarray(onek))


def _bias_terms(lf):
    b, t, w = lf.shape
    selq, selk, oneq, onek = _bias_layout()
    seq = pl.BlockSpec((None, t, w), lambda i: (i, 0, 0))
    return pl.pallas_call(
        _bias_terms_kernel,
        grid=(b,),
        in_specs=[seq, _resident((3, LANES, w)), _resident((3, LANES, w)),
                  _resident((1, w)), _resident((1, w))],
        out_specs=[seq, seq],
        out_shape=[jax.ShapeDtypeStruct((b, t, w), BF16), jax.ShapeDtypeStruct((b, t, w), BF16)],
        compiler_params=_cparams("parallel"),
        name="forget_bias_terms",
    )(lf, selq, selk, oneq, onek)


def _attn_kernel(q_ref, qa_ref, k_ref, ka_ref, v_ref, o_ref, q2_buf, s_buf, r_buf, m_buf, acc_buf,
                 *, tq):
    n_q = q_ref.shape[0] // tq
    lane = lax.broadcasted_iota(jnp.int32, (1, LANES), 1)
    ones = jnp.ones((tq, LANES), BF16)
    rc = ATTN_ROW_CHUNK
    assert tq % rc == 0, "a row chunk must stay inside one head of the stacked tile"
    chunks = [slice(r0, r0 + rc) for r0 in range(0, 2 * tq, rc)]
    n_tiles = tq // LANES
    ri = lax.broadcasted_iota(jnp.int32, (rc, tq), 0)
    ci = lax.broadcasted_iota(jnp.int32, (rc, tq), 1)

    def key_rows(j):
        return pl.ds(pl.multiple_of(j * tq, tq), tq)

    def tile_rows(qi, ch):
        return pl.ds(pl.multiple_of(qi * tq + (ch.start & (tq - 1)), rc), rc)

    def stacked_q(qi, ch):
        p = ch.start // tq
        q = q_ref[tile_rows(qi, ch), :]
        qa = qa_ref[tile_rows(qi, ch), :]
        head = (lane >> 6) == p
        b0 = BIAS_TERMS * (2 * pl.program_id(1) + p)
        bias = (lane >= b0) & (lane < b0 + BIAS_TERMS)
        zero = jnp.zeros_like(q)
        return jnp.concatenate([jnp.where(head, q, zero), jnp.where(bias, qa, zero)], axis=1)

    def logits(rows, ch):
        kf = jnp.concatenate([k_ref[rows, :], ka_ref[rows, :]], axis=1)
        return lax.dot_general(q2_buf[ch, :], kf, (((1,), (1,)), ((), ())),
                               preferred_element_type=F32)

    def row_max(s):
        return jnp.broadcast_to(jnp.max(s, axis=-1, keepdims=True), (s.shape[0], LANES))

    def absorb(ch, rows):
        vf = jnp.concatenate([v_ref[rows, :], ones], axis=1)
        s, m = s_buf[ch, :], m_buf[ch, :]
        m_new = jnp.maximum(m, r_buf[ch, :])
        alpha = jnp.exp2(m - m_new)
        pm = jnp.concatenate([jnp.exp2(s[:, i * LANES:(i + 1) * LANES] - m_new).astype(BF16)
                              for i in range(n_tiles)], axis=1)
        pv = jnp.dot(pm, vf, preferred_element_type=F32)
        return m_new, jnp.concatenate([alpha, alpha], axis=1) * acc_buf[ch, :] + pv

    def refill(j, ch):
        sn = logits(key_rows(j), ch)
        s_buf[ch, :] = sn
        r_buf[ch, :] = row_max(sn)

    def reset(ch):
        m_buf[ch, :] = jnp.full((rc, LANES), NEG_BIG, F32)
        acc_buf[ch, :] = jnp.zeros((rc, 2 * LANES), F32)

    for ch in chunks:
        q2_buf[ch, :] = stacked_q(0, ch)
        reset(ch)
        refill(0, ch)

    def tile(qi, carry):
        def step(j):
            for ch in chunks:
                m_new, acc_new = absorb(ch, key_rows(j))
                m_buf[ch, :] = m_new
                acc_buf[ch, :] = acc_new
                refill(j + 1, ch)

        def quad_step(i, c):
            for u in range(4):
                step(4 * i + u)
            return c

        lax.fori_loop(0, qi >> 2, quad_step, 0)
        done = qi & ~3

        @pl.when((qi & 2) != 0)
        def _():
            step(done)
            step(done + 1)

        @pl.when((qi & 1) != 0)
        def _():
            step(qi - 1)

        for ch in chunks:
            s = jnp.where(ri + (ch.start & (tq - 1)) >= ci, s_buf[ch, :], NEG_BIG)
            s_buf[ch, :] = s
            r_buf[ch, :] = row_max(s)
        q_next = jnp.minimum(qi + 1, n_q - 1)
        for ch in chunks:
            _, acc = absorb(ch, key_rows(qi))
            o = (acc[:, :LANES] / acc[:, LANES:]).astype(o_ref.dtype)
            if ch.start < tq:
                o_ref[tile_rows(qi, ch), :] = o
            else:
                o_ref[tile_rows(qi, ch), :] = jnp.where((lane >> 6) == 1, o,
                                                        o_ref[tile_rows(qi, ch), :])
            reset(ch)
            q2_buf[ch, :] = stacked_q(q_next, ch)
            refill(0, ch)
        return carry

    lax.fori_loop(0, n_q, tile, 0)


def _fox_attention(q, qa, k, ka, v, tq):
    b, t, w = q.shape
    n_pairs = w // LANES
    slab = pl.BlockSpec((None, t, LANES), lambda i, p: (i, 0, p))
    bias_slab = pl.BlockSpec((None, t, LANES), lambda i, p: (i, 0, 0))
    return pl.pallas_call(
        functools.partial(_attn_kernel, tq=tq),
        grid=(b, n_pairs),
        in_specs=[slab, bias_slab, slab, bias_slab, slab],
        out_specs=slab,
        out_shape=jax.ShapeDtypeStruct((b, t, w), BF16),
        scratch_shapes=[pltpu.VMEM((2 * tq, 2 * LANES), BF16),
                        pltpu.VMEM((2 * tq, tq), F32),
                        pltpu.VMEM((2 * tq, LANES), F32),
                        pltpu.VMEM((2 * tq, LANES), F32),
                        pltpu.VMEM((2 * tq, 2 * LANES), F32)],
        compiler_params=_cparams("parallel", "parallel"),
        name="fox_attention",
    )(q, qa, k, ka, v)


def _pad_lanes(a):
    return jnp.pad(a, [(0, 0)] * (a.ndim - 1) + [(0, LANES - a.shape[-1])])


def kernel(x, pre_mix_norm, post_mix_norm, pre_ffn_norm, post_ffn_norm, w_ffn_in, w_ffn_out,
           gdn_w_in, gdn_conv, gdn_a_log, gdn_dt_bias, gdn_out_norm, gdn_w_out,
           kv_norm, w_kv, b_forget, fox_w_q, fox_w_o):
    b, t, d = x.shape
    n = b * t
    assert gdn_w_in.shape[0] == 1 and fox_w_q.shape[0] == 1, "one layer of each mixer"
    x2 = x.reshape(n, d)
    row = lambda a: a.reshape(1, -1).astype(F32)

    w_ffn_in_b = w_ffn_in.astype(BF16)
    w_ffn_out_b = w_ffn_out.astype(BF16)

    def mix_ffn(mix, w_mix, x2, layer):
        return _mix_ffn(mix, w_mix.astype(BF16), x2, row(post_mix_norm[layer]),
                        row(pre_ffn_norm[layer]), w_ffn_in_b, w_ffn_out_b,
                        row(post_ffn_norm[layer]), layer, ROW_TILE, FFN_SPLITS)

    qkv_w = 3 * GDN_HEADS * GDN_DK
    z_w = GDN_HEADS * GDN_DV
    w_in = gdn_w_in[0]
    w_small = _pad_lanes(w_in[:, qkv_w + z_w:]).astype(BF16)
    qkv, z, gates = _gdn_in_proj(x2, row(pre_mix_norm[0]), w_in.astype(BF16), qkv_w, z_w, w_small,
                                 gdn_conv[0].astype(F32), GDN_PROJ_ROWS, t)
    zeros_h = jnp.zeros((GDN_HEADS,), F32)
    alog_pad = _pad_lanes(jnp.concatenate([zeros_h, gdn_a_log[0].astype(F32)])[None, :])
    dtb_pad = _pad_lanes(jnp.concatenate([zeros_h, gdn_dt_bias[0].astype(F32)])[None, :])
    og = _gdn_core(qkv.reshape(b, t, qkv_w), z.reshape(b, t, z_w), gates.reshape(b, t, LANES),
                   alog_pad, dtb_pad, row(gdn_out_norm[0]), GDN_TIME_BLOCK, GDN_CHUNKS_PER_PREP)
    x2 = mix_ffn(og.reshape(n, -1), gdn_w_out[0], x2, 0)

    fw = FOX_HEADS * FOX_HD
    q, k, v, lf = _fox_proj(x2, row(pre_mix_norm[1]), row(kv_norm), fox_w_q[0].astype(BF16),
                            w_kv.astype(BF16), _pad_lanes(w_kv[:, 2 * fw:]).astype(BF16),
                            _pad_lanes(b_forget.astype(F32)[None, :]), ROW_TILE)
    qa, ka = _bias_terms(lf.reshape(b, t, LANES))
    to3 = lambda a: a.reshape(b, t, -1)
    oa = _fox_attention(to3(q), to3(qa), to3(k), to3(ka), to3(v), ATTN_TILE)
    x2 = mix_ffn(oa.reshape(n, fw), fox_w_o[0], x2, 1)
    return x2.reshape(b, t, d)
```
